```python
import jax, jax.numpy as jnp
from jax import lax
import numpy as np

D_MODEL = 1024
BATCH = 4
SEQ = 8192
DEPTH = 4
DEC_BATCH = 8
DEC_SEQ = 32
PAST_LEN = 1024

CHUNK = 64
MIX_WIDTH = D_MODEL
SB_WIDTH = MIX_WIDTH // 2
SB_HEAD_DIM = 64
N_SB_HEADS = SB_WIDTH // SB_HEAD_DIM
POOL_WIDTH = MIX_WIDTH - SB_WIDTH
POOL_WINDOWS = (2, 4, 8, 16)
N_POOL_GROUPS = len(POOL_WINDOWS)
POOL_GROUP_DIM = POOL_WIDTH // N_POOL_GROUPS
POOL_HIST = 15
D_FF = ((8 * D_MODEL // 3 + 255) // 256) * 256
PLE_DIM = 256
Q_BLOCK = 128
EPS = 1e-6

kernel_name = 'stickbreak_pool_hybrid_stream_step'


def _rmsnorm(x, g):
    xf = x.astype(jnp.float32)
    y = xf * lax.rsqrt(jnp.mean(xf * xf, axis=-1, keepdims=True) + EPS)
    return (y * g.astype(jnp.float32)).astype(x.dtype)


def _stick_breaking(q, k, v, q_pos, k_pos):
    b, h, sq, dh = q.shape
    blk = min(Q_BLOCK, sq)
    nb = sq // blk
    scale = dh ** -0.5
    kf = k.astype(jnp.float32)
    vf = v.astype(jnp.float32)
    qb = jnp.moveaxis(q.reshape(b, h, nb, blk, dh), 2, 0)
    pb = q_pos.reshape(nb, blk)

    def one_block(args):
        qi, pi = args
        z = jnp.einsum('bhqd,bhkd->bhqk', qi.astype(jnp.float32), kf) * scale
        mask = k_pos[None, :] < pi[:, None]
        log_keep = jnp.where(mask, jax.nn.log_sigmoid(-z), 0.0)
        suffix = lax.cumsum(log_keep, axis=3, reverse=True) - log_keep
        a = jnp.where(mask, jnp.exp(jax.nn.log_sigmoid(z) + suffix), 0.0)
        return jnp.einsum('bhqk,bhkd->bhqd', a, vf)

    out = lax.map(one_block, (qb, pb))
    return jnp.moveaxis(out, 0, 2).reshape(b, h, sq, dh).astype(v.dtype)


def _multiscale_pool(u, hist, pos, w_pool, pool_scale):
    b, s, c = u.shape
    full = jnp.concatenate([hist.astype(u.dtype), u], axis=1)
    fullf = full.astype(jnp.float32)
    cs = jnp.concatenate([jnp.zeros((b, 1, c), jnp.float32), jnp.cumsum(fullf, axis=1)], axis=1)
    end = POOL_HIST + 1
    uf = u.astype(jnp.float32)
    diffs = []
    for g, w in enumerate(POOL_WINDOWS):
        lo, hi = g * POOL_GROUP_DIM, (g + 1) * POOL_GROUP_DIM
        win_sum = cs[:, end:end + s, lo:hi] - cs[:, end - w:end - w + s, lo:hi]
        cnt = jnp.minimum(w, pos + 1).astype(jnp.float32)[None, :, None]
        diffs.append(win_sum / cnt - uf[:, :, lo:hi])
    d = jnp.stack(diffs, axis=2)
    y = jnp.einsum('bsgc,gcd->bsgd', d, w_pool.astype(jnp.float32)).reshape(b, s, c)
    y = _rmsnorm(y, pool_scale).astype(u.dtype)
    return y, full[:, -POOL_HIST:]


def _layer(x, p, past_k, past_v, pool_hist, q_pos, k_pos,
           g_mix, w_in, g_sb_out, w_pool, pool_scale, w_out,
           g_ffn, w_ffn_gate, w_ffn_up, w_ffn_down, g_ple, w_ple, w_ple_gate):
    b, s, _ = x.shape
    h = _rmsnorm(x, g_mix)
    u = h @ w_in
    q, k, v, up = jnp.split(u, [SB_WIDTH, 2 * SB_WIDTH, 3 * SB_WIDTH], axis=-1)
    q = q.reshape(b, s, N_SB_HEADS, SB_HEAD_DIM).transpose(0, 2, 1, 3)
    k = k.reshape(b, s, N_SB_HEADS, SB_HEAD_DIM).transpose(0, 2, 1, 3)
    v = v.reshape(b, s, N_SB_HEADS, SB_HEAD_DIM).transpose(0, 2, 1, 3)
    if past_k is None:
        k_all, v_all = k, v
    else:
        k_all = jnp.concatenate([past_k.astype(k.dtype), k], axis=2)
        v_all = jnp.concatenate([past_v.astype(v.dtype), v], axis=2)
    att = _stick_breaking(q, k_all, v_all, q_pos, k_pos)
    att = _rmsnorm(att.transpose(0, 2, 1, 3).reshape(b, s, SB_WIDTH), g_sb_out)
    pool, new_hist = _multiscale_pool(up, pool_hist, q_pos, w_pool, pool_scale)
    x = x + jnp.concatenate([att, pool], axis=-1) @ w_out
    h2 = _rmsnorm(x, g_ffn)
    x = x + (jax.nn.silu(h2 @ w_ffn_gate) * (h2 @ w_ffn_up)) @ w_ffn_down
    gate = jax.nn.sigmoid(_rmsnorm(x, g_ple) @ w_ple_gate)
    x = x + (p @ w_ple) * gate
    return x, k, v, new_hist


def setup_inputs(seed: int = 0) -> dict:
    key = jax.random.key(seed)
    ks = jax.random.split(key, 24)
    f32 = jnp.float32

    def nrm(k, shape, scale):
        return jax.random.normal(k, shape, f32) * scale

    def gain(k, shape):
        return 1.0 + 0.02 * jax.random.normal(k, shape, f32)

    return {
        'x_prompt': nrm(ks[0], (BATCH, SEQ, D_MODEL), 1.0),
        'x_sample': nrm(ks[1], (DEC_BATCH, DEC_SEQ, D_MODEL), 1.0),
        'cache_k': nrm(ks[2], (DEPTH, DEC_BATCH, N_SB_HEADS, PAST_LEN, SB_HEAD_DIM), 1.0),
        'cache_v': nrm(ks[3], (DEPTH, DEC_BATCH, N_SB_HEADS, PAST_LEN, SB_HEAD_DIM), 1.0),
        'state_pool': nrm(ks[4], (DEPTH, DEC_BATCH, POOL_HIST, POOL_WIDTH), 1.0),
        'p_prompt': nrm(ks[5], (DEPTH, BATCH, SEQ, PLE_DIM), 1.0),
        'p_sample': nrm(ks[6], (DEPTH, DEC_BATCH, DEC_SEQ, PLE_DIM), 1.0),
        'g_mix': gain(ks[7], (DEPTH, D_MODEL)),
        'w_in': nrm(ks[8], (DEPTH, D_MODEL, 3 * SB_WIDTH + POOL_WIDTH), D_MODEL ** -0.5),
        'g_sb_out': gain(ks[9], (DEPTH, SB_WIDTH)),
        'w_pool': nrm(ks[10], (DEPTH, N_POOL_GROUPS, POOL_GROUP_DIM, POOL_GROUP_DIM), POOL_GROUP_DIM ** -0.5),
        'pool_scale': gain(ks[11], (DEPTH, POOL_WIDTH)),
        'w_out': nrm(ks[12], (DEPTH, MIX_WIDTH, D_MODEL), MIX_WIDTH ** -0.5),
        'g_ffn': gain(ks[13], (DEPTH, D_MODEL)),
        'w_ffn_gate': nrm(ks[14], (DEPTH, D_MODEL, D_FF), D_MODEL ** -0.5),
        'w_ffn_up': nrm(ks[15], (DEPTH, D_MODEL, D_FF), D_MODEL ** -0.5),
        'w_ffn_down': nrm(ks[16], (DEPTH, D_FF, D_MODEL), D_FF ** -0.5),
        'g_ple': gain(ks[17], (DEPTH, D_MODEL)),
        'w_ple': nrm(ks[18], (DEPTH, PLE_DIM, D_MODEL), PLE_DIM ** -0.5),
        'w_ple_gate': nrm(ks[19], (DEPTH, D_MODEL, D_MODEL), D_MODEL ** -0.5),
        'g_final': gain(ks[20], (D_MODEL,)),
    }


def reference(x_prompt, x_sample, cache_k, cache_v, state_pool, p_prompt, p_sample,
              g_mix, w_in, g_sb_out, w_pool, pool_scale, w_out,
              g_ffn, w_ffn_gate, w_ffn_up, w_ffn_down, g_ple, w_ple, w_ple_gate, g_final):
    past = cache_k.shape[3]
    seq = x_prompt.shape[1]
    dseq = x_sample.shape[1]
    q_pos_p = jnp.arange(seq, dtype=jnp.int32)
    k_pos_p = q_pos_p
    q_pos_s = past + jnp.arange(dseq, dtype=jnp.int32)
    k_pos_s = jnp.arange(past + dseq, dtype=jnp.int32)
    hist_p = jnp.zeros((x_prompt.shape[0], POOL_HIST, POOL_WIDTH), x_prompt.dtype)

    xp, xs = x_prompt, x_sample
    kp_l, vp_l, sp_l, ks_l, vs_l, ss_l = [], [], [], [], [], []
    for l in range(DEPTH):
        lw = (g_mix[l], w_in[l], g_sb_out[l], w_pool[l], pool_scale[l], w_out[l],
              g_ffn[l], w_ffn_gate[l], w_ffn_up[l], w_ffn_down[l], g_ple[l], w_ple[l], w_ple_gate[l])
        xp, kp, vp, sp = _layer(xp, p_prompt[l], None, None, hist_p, q_pos_p, k_pos_p, *lw)
        xs, kk, vv, ss = _layer(xs, p_sample[l], cache_k[l], cache_v[l], state_pool[l],
                                q_pos_s, k_pos_s, *lw)
        kp_l.append(kp); vp_l.append(vp); sp_l.append(sp)
        ks_l.append(kk); vs_l.append(vv); ss_l.append(ss)

    y_prompt = _rmsnorm(xp, g_final)
    y_sample = _rmsnorm(xs, g_final)
    return (y_prompt, y_sample,
            jnp.stack(kp_l), jnp.stack(vp_l), jnp.stack(sp_l),
            jnp.stack(ks_l), jnp.stack(vs_l), jnp.stack(ss_l))
```

```python
import functools
import math

import jax
import jax.numpy as jnp
from jax import lax
from jax.experimental import pallas as pl
from jax.experimental.pallas import tpu as pltpu

F32 = jnp.float32
BF16 = jnp.bfloat16

EPS = 1e-6
SB_HEAD_DIM = 64
POOL_WINDOWS = (2, 4, 8, 16)
POOL_HALO = 16
VMEM_LIMIT_BYTES = 56 * 1024 * 1024


def _rms(x, g):
    ms = jnp.mean(x * x, axis=-1, keepdims=True)
    return (x * lax.rsqrt(ms + EPS)) * g


def _const_spec(shape):
    return pl.BlockSpec(shape, lambda *_: (0,) * len(shape))


def _params(sem):
    return pltpu.CompilerParams(dimension_semantics=sem, vmem_limit_bytes=VMEM_LIMIT_BYTES)


def _in_proj_kernel(x_ref, g_ref, w_ref, q_ref, k_ref, v_ref, up_ref, *, bt, ts, nh, dh):
    d = x_ref.shape[-1]
    sbw = nh * dh
    x = x_ref[...].reshape(bt * ts, d)
    h = _rms(x, g_ref[...]).astype(BF16)
    for part, ref in enumerate((q_ref, k_ref, v_ref)):
        u = jnp.dot(h, w_ref[:, part * sbw:(part + 1) * sbw], preferred_element_type=F32)
        for b in range(bt):
            for hh in range(nh):
                ref[b, hh] = u[b * ts:(b + 1) * ts, hh * dh:(hh + 1) * dh]
    u = jnp.dot(h, w_ref[:, 3 * sbw:], preferred_element_type=F32)
    for b in range(bt):
        up_ref[b] = u[b * ts:(b + 1) * ts, :]


def _in_proj(x, g, w, *, bt, ts, nh, dh):
    b, s, d = x.shape
    sbw = nh * dh
    pw = w.shape[1] - 3 * sbw
    hm = jax.ShapeDtypeStruct((b, nh, s, dh), F32)
    hm_spec = pl.BlockSpec((bt, nh, ts, dh), lambda i, j: (i, 0, j, 0))
    return pl.pallas_call(
        functools.partial(_in_proj_kernel, bt=bt, ts=ts, nh=nh, dh=dh),
        grid=(b // bt, s // ts),
        in_specs=[pl.BlockSpec((bt, ts, d), lambda i, j: (i, j, 0)),
                  _const_spec((1, d)), _const_spec(w.shape)],
        out_specs=[hm_spec, hm_spec, hm_spec,
                   pl.BlockSpec((bt, ts, pw), lambda i, j: (i, j, 0))],
        out_shape=[hm, hm, hm, jax.ShapeDtypeStruct((b, s, pw), F32)],
        compiler_params=_params(("parallel", "parallel")),
        name="in_proj",
    )(x, g, w)


def _strict_lower(n):
    j = lax.broadcasted_iota(jnp.int32, (n, n), 0)
    s = lax.broadcasted_iota(jnp.int32, (n, n), 1)
    return (j > s).astype(BF16)


def _sb_tile(q, kb, vb, tmat, c, acc, mask):
    z = lax.dot_general(q, kb, (((1,), (1,)), ((), ())), preferred_element_type=F32)
    sp = jnp.maximum(z, 0.0) + jnp.log(1.0 + jnp.exp(-jnp.abs(z)))
    lb = z - sp
    if mask is not None:
        sp = jnp.where(mask, sp, 0.0)
    spb = sp.astype(BF16)
    pex = jnp.dot(spb, tmat, preferred_element_type=F32)
    a = jnp.exp(lb - pex - c)
    if mask is not None:
        a = jnp.where(mask, a, 0.0)
    acc = acc + jnp.dot(a.astype(BF16), vb, preferred_element_type=F32)
    c = c + pex[:, :1] + spb[:, :1].astype(F32)
    return c, acc


def _causal_mask(n):
    row = lax.broadcasted_iota(jnp.int32, (n, n), 0)
    col = lax.broadcasted_iota(jnp.int32, (n, n), 1)
    return col < row


def _attn_kernel(q_ref, k_ref, v_ref, t_ref, o_ref, *, tq, qscale):
    i = pl.program_id(2)
    dh = q_ref.shape[-1]
    q = (q_ref[0, 0] * qscale).astype(BF16)
    tmat = t_ref[...]

    def kv(j):
        start = pl.multiple_of(j * tq, tq)
        return (k_ref[0, 0, pl.ds(start, tq), :].astype(BF16),
                v_ref[0, 0, pl.ds(start, tq), :].astype(BF16))

    kb, vb = kv(i)
    c, acc = _sb_tile(q, kb, vb, tmat, jnp.zeros((tq, 1), F32), jnp.zeros((tq, dh), F32),
                      _causal_mask(tq))

    def body(n, carry):
        kb, vb = kv(i - 1 - n)
        return _sb_tile(q, kb, vb, tmat, carry[0], carry[1], None)

    c, acc = lax.fori_loop(0, i, body, (c, acc))
    o_ref[0, 0] = acc


def _attn(q, k, v, tmat, *, tq):
    b, nh, s, dh = q.shape
    return pl.pallas_call(
        functools.partial(_attn_kernel, tq=tq, qscale=dh ** -0.5),
        grid=(b, nh, s // tq),
        in_specs=[pl.BlockSpec((1, 1, tq, dh), lambda bi, hi, i: (bi, hi, i, 0)),
                  pl.BlockSpec((1, 1, s, dh), lambda bi, hi, i: (bi, hi, 0, 0)),
                  pl.BlockSpec((1, 1, s, dh), lambda bi, hi, i: (bi, hi, 0, 0)),
                  _const_spec(tmat.shape)],
        out_specs=pl.BlockSpec((1, 1, tq, dh), lambda bi, hi, i: (bi, hi, i, 0)),
        out_shape=jax.ShapeDtypeStruct((b, nh, s, dh), F32),
        compiler_params=_params(("parallel", "parallel", "arbitrary")),
        name="attn",
    )(q, k, v, tmat)


def _attn_dec_kernel(q_ref, kn_ref, vn_ref, kp_ref, vp_ref, tn_ref, tp_ref, o_ref, *, tkp, qscale):
    ts, dh = q_ref.shape[-2:]
    past = kp_ref.shape[-2]
    q = (q_ref[0, 0] * qscale).astype(BF16)
    c, acc = _sb_tile(q, kn_ref[0, 0].astype(BF16), vn_ref[0, 0].astype(BF16), tn_ref[...],
                      jnp.zeros((ts, 1), F32), jnp.zeros((ts, dh), F32), _causal_mask(ts))
    for j in reversed(range(past // tkp)):
        kb = kp_ref[0, 0, j * tkp:(j + 1) * tkp, :].astype(BF16)
        vb = vp_ref[0, 0, j * tkp:(j + 1) * tkp, :].astype(BF16)
        c, acc = _sb_tile(q, kb, vb, tp_ref[...], c, acc, None)
    o_ref[0, 0] = acc


def _attn_dec(q, kn, vn, kp, vp, tn, tp):
    b, nh, ts, dh = q.shape
    past = kp.shape[2]
    new_spec = pl.BlockSpec((1, 1, ts, dh), lambda bi, hi: (bi, hi, 0, 0))
    past_spec = pl.BlockSpec((1, 1, past, dh), lambda bi, hi: (bi, hi, 0, 0))
    return pl.pallas_call(
        functools.partial(_attn_dec_kernel, tkp=tp.shape[0], qscale=dh ** -0.5),
        grid=(b, nh),
        in_specs=[new_spec, new_spec, new_spec, past_spec, past_spec,
                  _const_spec(tn.shape), _const_spec(tp.shape)],
        out_specs=new_spec,
        out_shape=jax.ShapeDtypeStruct((b, nh, ts, dh), F32),
        compiler_params=_params(("parallel", "parallel")),
        name="attn_dec",
    )(q, kn, vn, kp, vp, tn, tp)


def _mix_kernel(x_ref, att_ref, up_ref, halo_ref, gsb_ref, wpool_ref, pscale_ref, wout_ref,
                o_ref, ext_ref, *, bt, ts, nh, pos_offset, zero_first_halo):
    j = pl.program_id(1)
    pw = up_ref.shape[-1]
    gd = pw // len(POOL_WINDOWS)
    pos = pos_offset + j * ts + lax.broadcasted_iota(jnp.int32, (ts, 1), 0)
    for b in range(bt):
        att = jnp.concatenate([att_ref[b, hh] for hh in range(nh)], axis=-1)
        att = _rms(att, gsb_ref[...]).astype(BF16)

        halo = halo_ref[b]
        if zero_first_halo:
            halo = jnp.where(j == 0, 0.0, halo)
        up = up_ref[b]
        ext_ref[0:POOL_HALO, :] = halo
        ext_ref[POOL_HALO:, :] = up
        ys = []
        for g, w in enumerate(POOL_WINDOWS):
            cols = slice(g * gd, (g + 1) * gd)
            win = up[:, cols]
            for sh in range(1, w):
                win = win + ext_ref[POOL_HALO - sh:POOL_HALO - sh + ts, cols]
            cnt = jnp.minimum(w, pos + 1).astype(F32)
            dg = win / cnt - up[:, cols]
            ys.append(jnp.dot(dg.astype(BF16), wpool_ref[g], preferred_element_type=F32))
        pool = _rms(jnp.concatenate(ys, axis=-1), pscale_ref[...]).astype(BF16)

        mixed = jnp.concatenate([att, pool], axis=-1)
        o_ref[b] = x_ref[b] + jnp.dot(mixed, wout_ref[...], preferred_element_type=F32)


def _mix(x, att, up, halo_src, gsb, wpool, pscale, wout, *, bt, ts, pos_offset, zero_first_halo):
    b, s, d = x.shape
    nh, dh = att.shape[1], att.shape[3]
    pw = up.shape[-1]
    if zero_first_halo:
        halo_map = lambda i, j: (i, jnp.maximum(j * (ts // POOL_HALO) - 1, 0), 0)
    else:
        halo_map = lambda i, j: (i, 0, 0)
    return pl.pallas_call(
        functools.partial(_mix_kernel, bt=bt, ts=ts, nh=nh, pos_offset=pos_offset,
                          zero_first_halo=zero_first_halo),
        grid=(b // bt, s // ts),
        in_specs=[pl.BlockSpec((bt, ts, d), lambda i, j: (i, j, 0)),
                  pl.BlockSpec((bt, nh, ts, dh), lambda i, j: (i, 0, j, 0)),
                  pl.BlockSpec((bt, ts, pw), lambda i, j: (i, j, 0)),
                  pl.BlockSpec((bt, POOL_HALO, pw), halo_map),
                  _const_spec(gsb.shape), _const_spec(wpool.shape),
                  _const_spec(pscale.shape), _const_spec(wout.shape)],
        out_specs=pl.BlockSpec((bt, ts, d), lambda i, j: (i, j, 0)),
        out_shape=jax.ShapeDtypeStruct((b, s, d), F32),
        scratch_shapes=[pltpu.VMEM((POOL_HALO + ts, pw), F32)],
        compiler_params=_params(("parallel", "arbitrary")),
        name="mix",
    )(x, att, up, halo_src, gsb, wpool, pscale, wout)


def _ffn_kernel(x_ref, p_ref, gffn_ref, wg_ref, wu_ref, wd_ref, gple_ref, wple_ref, wpg_ref,
                gfin_ref, o_ref, *, final_norm):
    x = x_ref[...]
    h = _rms(x, gffn_ref[...]).astype(BF16)
    gate = jnp.dot(h, wg_ref[...], preferred_element_type=F32)
    upv = jnp.dot(h, wu_ref[...], preferred_element_type=F32)
    hid = (gate * jax.nn.sigmoid(gate) * upv).astype(BF16)
    x = x + jnp.dot(hid, wd_ref[...], preferred_element_type=F32)
    hp = _rms(x, gple_ref[...]).astype(BF16)
    pg = jax.nn.sigmoid(jnp.dot(hp, wpg_ref[...], preferred_element_type=F32))
    x = x + jnp.dot(p_ref[...].astype(BF16), wple_ref[...], preferred_element_type=F32) * pg
    if final_norm:
        x = _rms(x, gfin_ref[...])
    o_ref[...] = x


def _ffn(x, p, gffn, wg, wu, wd, gple, wple, wpg, gfin, *, tm, final_norm):
    t, d = x.shape
    pd = p.shape[-1]
    return pl.pallas_call(
        functools.partial(_ffn_kernel, final_norm=final_norm),
        grid=(t // tm,),
        in_specs=[pl.BlockSpec((tm, d), lambda i: (i, 0)),
                  pl.BlockSpec((tm, pd), lambda i: (i, 0)),
                  _const_spec(gffn.shape), _const_spec(wg.shape), _const_spec(wu.shape),
                  _const_spec(wd.shape), _const_spec(gple.shape), _const_spec(wple.shape),
                  _const_spec(wpg.shape), _const_spec(gfin.shape)],
        out_specs=pl.BlockSpec((tm, d), lambda i: (i, 0)),
        out_shape=jax.ShapeDtypeStruct((t, d), F32),
        compiler_params=_params(("parallel",)),
        name="ffn",
    )(x, p, gffn, wg, wu, wd, gple, wple, wpg, gfin)


def _forward(x_prompt, x_sample, cache_k, cache_v, state_pool, p_prompt, p_sample,
             g_mix, w_in, g_sb_out, w_pool, pool_scale, w_out,
             g_ffn, w_ffn_gate, w_ffn_up, w_ffn_down, g_ple, w_ple, w_ple_gate, g_final,
             *, ts_prompt, tq, tm_prompt, tkp):
    depth = w_in.shape[0]
    bp, sp_len, d = x_prompt.shape
    bs, ss_len, _ = x_sample.shape
    nh = cache_k.shape[2]
    dh = cache_k.shape[4]
    past = cache_k.shape[3]
    hist = state_pool.shape[2]

    row = lambda a: a.reshape(1, -1)
    w_in_b, w_pool_b, w_out_b = w_in.astype(BF16), w_pool.astype(BF16), w_out.astype(BF16)
    wg_b, wu_b, wd_b = w_ffn_gate.astype(BF16), w_ffn_up.astype(BF16), w_ffn_down.astype(BF16)
    wple_b, wpg_b = w_ple.astype(BF16), w_ple_gate.astype(BF16)
    t_prompt, t_new, t_past = _strict_lower(tq), _strict_lower(ss_len), _strict_lower(tkp)
    state_pad = jnp.pad(state_pool, ((0, 0), (0, 0), (POOL_HALO - hist, 0), (0, 0)))
    gfin = row(g_final)

    xp, xs = x_prompt, x_sample
    outs = [[] for _ in range(6)]
    for l in range(depth):
        last = l == depth - 1
        qp, kp, vp, upp = _in_proj(xp, row(g_mix[l]), w_in_b[l], bt=1, ts=ts_prompt, nh=nh, dh=dh)
        attp = _attn(qp, kp, vp, t_prompt, tq=tq)
        xp = _mix(xp, attp, upp, upp, row(g_sb_out[l]), w_pool_b[l], row(pool_scale[l]), w_out_b[l],
                  bt=1, ts=ts_prompt, pos_offset=0, zero_first_halo=True)
        xp = _ffn(xp.reshape(bp * sp_len, d), p_prompt[l].reshape(bp * sp_len, -1),
                  row(g_ffn[l]), wg_b[l], wu_b[l], wd_b[l], row(g_ple[l]), wple_b[l], wpg_b[l], gfin,
                  tm=tm_prompt, final_norm=last).reshape(bp, sp_len, d)
        qs, ks, vs, ups = _in_proj(xs, row(g_mix[l]), w_in_b[l], bt=bs, ts=ss_len, nh=nh, dh=dh)
        atts = _attn_dec(qs, ks, vs, cache_k[l], cache_v[l], t_new, t_past)
        xs = _mix(xs, atts, ups, state_pad[l], row(g_sb_out[l]), w_pool_b[l], row(pool_scale[l]),
                  w_out_b[l], bt=bs, ts=ss_len, pos_offset=past, zero_first_halo=False)
        xs = _ffn(xs.reshape(bs * ss_len, d), p_sample[l].reshape(bs * ss_len, -1),
                  row(g_ffn[l]), wg_b[l], wu_b[l], wd_b[l], row(g_ple[l]), wple_b[l], wpg_b[l], gfin,
                  tm=bs * ss_len, final_norm=last).reshape(bs, ss_len, d)

        new_hist_s = jnp.concatenate([state_pool[l], ups], axis=1)[:, -hist:]
        for lst, val in zip(outs, (kp, vp, upp[:, -hist:], ks, vs, new_hist_s)):
            lst.append(val)

    return (xp, xs) + tuple(jnp.stack(o) for o in outs)


def kernel(x_prompt, x_sample, cache_k, cache_v, state_pool, p_prompt, p_sample, g_mix, w_in, g_sb_out, w_pool, pool_scale, w_out, g_ffn, w_ffn_gate, w_ffn_up, w_ffn_down, g_ple, w_ple, w_ple_gate, g_final):
    return _forward(x_prompt, x_sample, cache_k, cache_v, state_pool, p_prompt, p_sample,
                    g_mix, w_in, g_sb_out, w_pool, pool_scale, w_out,
                    g_ffn, w_ffn_gate, w_ffn_up, w_ffn_down, g_ple, w_ple, w_ple_gate, g_final,
                    ts_prompt=512, tq=256, tm_prompt=256, tkp=256)
```

```python
import functools
import math

import jax
import jax.numpy as jnp
from jax import lax
from jax.experimental import pallas as pl
from jax.experimental.pallas import tpu as pltpu

F32 = jnp.float32
BF16 = jnp.bfloat16

EPS = 1e-6
SB_HEAD_DIM = 64
POOL_WINDOWS = (2, 4, 8, 16)
POOL_HALO = 16
VMEM_LIMIT_BYTES = 56 * 1024 * 1024
EXP_UNDERFLOW = 110.0


def _rms(x, g):
    ms = jnp.mean(x * x, axis=-1, keepdims=True)
    return (x * lax.rsqrt(ms + EPS)) * g


def _const_spec(shape):
    return pl.BlockSpec(shape, lambda *_: (0,) * len(shape))


def _params(sem):
    return pltpu.CompilerParams(dimension_semantics=sem, vmem_limit_bytes=VMEM_LIMIT_BYTES)


def _in_proj_kernel(x_ref, g_ref, w_ref, q_ref, k_ref, v_ref, up_ref, *, bt, ts, nh, dh):
    d = x_ref.shape[-1]
    sbw = nh * dh
    x = x_ref[...].reshape(bt * ts, d)
    h = _rms(x, g_ref[...]).astype(BF16)
    for part, ref in enumerate((q_ref, k_ref, v_ref)):
        u = jnp.dot(h, w_ref[:, part * sbw:(part + 1) * sbw], preferred_element_type=F32)
        for b in range(bt):
            for hh in range(nh):
                ref[b, hh] = u[b * ts:(b + 1) * ts, hh * dh:(hh + 1) * dh]
    u = jnp.dot(h, w_ref[:, 3 * sbw:], preferred_element_type=F32)
    for b in range(bt):
        up_ref[b] = u[b * ts:(b + 1) * ts, :]


def _in_proj(x, g, w, *, bt, ts, nh, dh):
    b, s, d = x.shape
    sbw = nh * dh
    pw = w.shape[1] - 3 * sbw
    hm = jax.ShapeDtypeStruct((b, nh, s, dh), F32)
    hm_spec = pl.BlockSpec((bt, nh, ts, dh), lambda i, j: (i, 0, j, 0))
    return pl.pallas_call(
        functools.partial(_in_proj_kernel, bt=bt, ts=ts, nh=nh, dh=dh),
        grid=(b // bt, s // ts),
        in_specs=[pl.BlockSpec((bt, ts, d), lambda i, j: (i, j, 0)),
                  _const_spec((1, d)), _const_spec(w.shape)],
        out_specs=[hm_spec, hm_spec, hm_spec,
                   pl.BlockSpec((bt, ts, pw), lambda i, j: (i, j, 0))],
        out_shape=[hm, hm, hm, jax.ShapeDtypeStruct((b, s, pw), F32)],
        compiler_params=_params(("parallel", "parallel")),
        name="in_proj",
    )(x, g, w)


def _strict_lower(n):
    j = lax.broadcasted_iota(jnp.int32, (n, n), 0)
    s = lax.broadcasted_iota(jnp.int32, (n, n), 1)
    return (j > s).astype(BF16)


def _sb_tile(q, kb, vb, tmat, c, acc, mask):
    z = lax.dot_general(q, kb, (((1,), (1,)), ((), ())), preferred_element_type=F32)
    sp = jnp.maximum(z, 0.0) + jnp.log(1.0 + jnp.exp(-jnp.abs(z)))
    lb = z - sp
    if mask is not None:
        sp = jnp.where(mask, sp, 0.0)
    spb = sp.astype(BF16)
    pex = jnp.dot(spb, tmat, preferred_element_type=F32)
    a = jnp.exp(lb - pex - c)
    if mask is not None:
        a = jnp.where(mask, a, 0.0)
    acc = acc + jnp.dot(a.astype(BF16), vb, preferred_element_type=F32)
    c = c + pex[:, :1] + spb[:, :1].astype(F32)
    return c, acc


def _causal_mask(n):
    row = lax.broadcasted_iota(jnp.int32, (n, n), 0)
    col = lax.broadcasted_iota(jnp.int32, (n, n), 1)
    return col < row


def _attn_kernel(q_ref, k_ref, v_ref, t_ref, o_ref, *, tq, hp, qscale):
    i = pl.program_id(2)
    dh = q_ref.shape[-1]
    tmat = t_ref[...]
    qs = [(q_ref[0, h] * qscale).astype(BF16) for h in range(hp)]
    mask = _causal_mask(tq)

    def kv(h, j):
        start = pl.multiple_of(j * tq, tq)
        return (k_ref[0, h, pl.ds(start, tq), :].astype(BF16),
                v_ref[0, h, pl.ds(start, tq), :].astype(BF16))

    def diag(h):
        kb, vb = kv(h, i)
        return _sb_tile(qs[h], kb, vb, tmat, jnp.zeros((tq, 1), F32), jnp.zeros((tq, dh), F32), mask)

    def step(h, j, c, acc):
        kb, vb = kv(h, j)
        return _sb_tile(qs[h], kb, vb, tmat, c, acc, None)

    def cmin_of(state):
        return functools.reduce(jnp.minimum, [jnp.min(c) for c, _ in state])

    state = lax.cond(i == 0,
                     lambda: tuple(diag(h) for h in range(hp)),
                     lambda: tuple(step(h, i - 1, *diag(h)) for h in range(hp)))

    def cond(st):
        return jnp.logical_and(st[0] >= 0, st[1] < EXP_UNDERFLOW)

    def body(st):
        j, _, state = st
        state = tuple(step(h, j, *state[h]) for h in range(hp))
        return j - 1, cmin_of(state), state

    _, _, state = lax.while_loop(cond, body, (i - 2, cmin_of(state), state))
    for h in range(hp):
        o_ref[0, h] = state[h][1]


def _attn(q, k, v, tmat, *, tq, hp):
    b, nh, s, dh = q.shape
    return pl.pallas_call(
        functools.partial(_attn_kernel, tq=tq, hp=hp, qscale=dh ** -0.5),
        grid=(b, nh // hp, s // tq),
        in_specs=[pl.BlockSpec((1, hp, tq, dh), lambda bi, hi, i: (bi, hi, i, 0)),
                  pl.BlockSpec((1, hp, s, dh), lambda bi, hi, i: (bi, hi, 0, 0)),
                  pl.BlockSpec((1, hp, s, dh), lambda bi, hi, i: (bi, hi, 0, 0)),
                  _const_spec(tmat.shape)],
        out_specs=pl.BlockSpec((1, hp, tq, dh), lambda bi, hi, i: (bi, hi, i, 0)),
        out_shape=jax.ShapeDtypeStruct((b, nh, s, dh), F32),
        compiler_params=_params(("parallel", "parallel", "arbitrary")),
        name="attn",
    )(q, k, v, tmat)


def _attn_dec_kernel(q_ref, kn_ref, vn_ref, kp_ref, vp_ref, tn_ref, tp_ref, o_ref, *, tkp, qscale):
    ts, dh = q_ref.shape[-2:]
    past = kp_ref.shape[-2]
    q = (q_ref[0, 0] * qscale).astype(BF16)
    c, acc = _sb_tile(q, kn_ref[0, 0].astype(BF16), vn_ref[0, 0].astype(BF16), tn_ref[...],
                      jnp.zeros((ts, 1), F32), jnp.zeros((ts, dh), F32), _causal_mask(ts))
    for j in reversed(range(past // tkp)):
        kb = kp_ref[0, 0, j * tkp:(j + 1) * tkp, :].astype(BF16)
        vb = vp_ref[0, 0, j * tkp:(j + 1) * tkp, :].astype(BF16)
        c, acc = _sb_tile(q, kb, vb, tp_ref[...], c, acc, None)
    o_ref[0, 0] = acc


def _attn_dec(q, kn, vn, kp, vp, tn, tp):
    b, nh, ts, dh = q.shape
    past = kp.shape[2]
    new_spec = pl.BlockSpec((1, 1, ts, dh), lambda bi, hi: (bi, hi, 0, 0))
    past_spec = pl.BlockSpec((1, 1, past, dh), lambda bi, hi: (bi, hi, 0, 0))
    return pl.pallas_call(
        functools.partial(_attn_dec_kernel, tkp=tp.shape[0], qscale=dh ** -0.5),
        grid=(b, nh),
        in_specs=[new_spec, new_spec, new_spec, past_spec, past_spec,
                  _const_spec(tn.shape), _const_spec(tp.shape)],
        out_specs=new_spec,
        out_shape=jax.ShapeDtypeStruct((b, nh, ts, dh), F32),
        compiler_params=_params(("parallel", "parallel")),
        name="attn_dec",
    )(q, kn, vn, kp, vp, tn, tp)


def _mix_kernel(x_ref, att_ref, up_ref, halo_ref, gsb_ref, wpool_ref, pscale_ref, wout_ref,
                o_ref, ext_ref, *, bt, ts, nh, pos_offset, zero_first_halo):
    j = pl.program_id(1)
    pw = up_ref.shape[-1]
    gd = pw // len(POOL_WINDOWS)
    pos = pos_offset + j * ts + lax.broadcasted_iota(jnp.int32, (ts, 1), 0)
    for b in range(bt):
        att = jnp.concatenate([att_ref[b, hh] for hh in range(nh)], axis=-1)
        att = _rms(att, gsb_ref[...]).astype(BF16)

        halo = halo_ref[b]
        if zero_first_halo:
            halo = jnp.where(j == 0, 0.0, halo)
        up = up_ref[b]
        ext_ref[0:POOL_HALO, :] = halo
        ext_ref[POOL_HALO:, :] = up
        ys = []
        for g, w in enumerate(POOL_WINDOWS):
            cols = slice(g * gd, (g + 1) * gd)
            win = up[:, cols]
            for sh in range(1, w):
                win = win + ext_ref[POOL_HALO - sh:POOL_HALO - sh + ts, cols]
            cnt = jnp.minimum(w, pos + 1).astype(F32)
            dg = win / cnt - up[:, cols]
            ys.append(jnp.dot(dg.astype(BF16), wpool_ref[g], preferred_element_type=F32))
        pool = _rms(jnp.concatenate(ys, axis=-1), pscale_ref[...]).astype(BF16)

        mixed = jnp.concatenate([att, pool], axis=-1)
        o_ref[b] = x_ref[b] + jnp.dot(mixed, wout_ref[...], preferred_element_type=F32)


def _mix(x, att, up, halo_src, gsb, wpool, pscale, wout, *, bt, ts, pos_offset, zero_first_halo):
    b, s, d = x.shape
    nh, dh = att.shape[1], att.shape[3]
    pw = up.shape[-1]
    if zero_first_halo:
        halo_map = lambda i, j: (i, jnp.maximum(j * (ts // POOL_HALO) - 1, 0), 0)
    else:
        halo_map = lambda i, j: (i, 0, 0)
    return pl.pallas_call(
        functools.partial(_mix_kernel, bt=bt, ts=ts, nh=nh, pos_offset=pos_offset,
                          zero_first_halo=zero_first_halo),
        grid=(b // bt, s // ts),
        in_specs=[pl.BlockSpec((bt, ts, d), lambda i, j: (i, j, 0)),
                  pl.BlockSpec((bt, nh, ts, dh), lambda i, j: (i, 0, j, 0)),
                  pl.BlockSpec((bt, ts, pw), lambda i, j: (i, j, 0)),
                  pl.BlockSpec((bt, POOL_HALO, pw), halo_map),
                  _const_spec(gsb.shape), _const_spec(wpool.shape),
                  _const_spec(pscale.shape), _const_spec(wout.shape)],
        out_specs=pl.BlockSpec((bt, ts, d), lambda i, j: (i, j, 0)),
        out_shape=jax.ShapeDtypeStruct((b, s, d), F32),
        scratch_shapes=[pltpu.VMEM((POOL_HALO + ts, pw), F32)],
        compiler_params=_params(("parallel", "arbitrary")),
        name="mix",
    )(x, att, up, halo_src, gsb, wpool, pscale, wout)


def _ffn_kernel(x_ref, p_ref, gffn_ref, wg_ref, wu_ref, wd_ref, gple_ref, wple_ref, wpg_ref,
                gfin_ref, o_ref, *, final_norm):
    x = x_ref[...]
    h = _rms(x, gffn_ref[...]).astype(BF16)
    gate = jnp.dot(h, wg_ref[...], preferred_element_type=F32)
    upv = jnp.dot(h, wu_ref[...], preferred_element_type=F32)
    hid = (gate * jax.nn.sigmoid(gate) * upv).astype(BF16)
    x = x + jnp.dot(hid, wd_ref[...], preferred_element_type=F32)
    hp = _rms(x, gple_ref[...]).astype(BF16)
    pg = jax.nn.sigmoid(jnp.dot(hp, wpg_ref[...], preferred_element_type=F32))
    x = x + jnp.dot(p_ref[...].astype(BF16), wple_ref[...], preferred_element_type=F32) * pg
    if final_norm:
        x = _rms(x, gfin_ref[...])
    o_ref[...] = x


def _ffn(x, p, gffn, wg, wu, wd, gple, wple, wpg, gfin, *, tm, final_norm):
    t, d = x.shape
    pd = p.shape[-1]
    return pl.pallas_call(
        functools.partial(_ffn_kernel, final_norm=final_norm),
        grid=(t // tm,),
        in_specs=[pl.BlockSpec((tm, d), lambda i: (i, 0)),
                  pl.BlockSpec((tm, pd), lambda i: (i, 0)),
                  _const_spec(gffn.shape), _const_spec(wg.shape), _const_spec(wu.shape),
                  _const_spec(wd.shape), _const_spec(gple.shape), _const_spec(wple.shape),
                  _const_spec(wpg.shape), _const_spec(gfin.shape)],
        out_specs=pl.BlockSpec((tm, d), lambda i: (i, 0)),
        out_shape=jax.ShapeDtypeStruct((t, d), F32),
        compiler_params=_params(("parallel",)),
        name="ffn",
    )(x, p, gffn, wg, wu, wd, gple, wple, wpg, gfin)


def _forward(x_prompt, x_sample, cache_k, cache_v, state_pool, p_prompt, p_sample,
             g_mix, w_in, g_sb_out, w_pool, pool_scale, w_out,
             g_ffn, w_ffn_gate, w_ffn_up, w_ffn_down, g_ple, w_ple, w_ple_gate, g_final,
             *, ts_prompt, tq, hp, tm_prompt, tkp):
    depth = w_in.shape[0]
    bp, sp_len, d = x_prompt.shape
    bs, ss_len, _ = x_sample.shape
    nh = cache_k.shape[2]
    dh = cache_k.shape[4]
    past = cache_k.shape[3]
    hist = state_pool.shape[2]

    row = lambda a: a.reshape(1, -1)
    w_in_b, w_pool_b, w_out_b = w_in.astype(BF16), w_pool.astype(BF16), w_out.astype(BF16)
    wg_b, wu_b, wd_b = w_ffn_gate.astype(BF16), w_ffn_up.astype(BF16), w_ffn_down.astype(BF16)
    wple_b, wpg_b = w_ple.astype(BF16), w_ple_gate.astype(BF16)
    t_prompt, t_new, t_past = _strict_lower(tq), _strict_lower(ss_len), _strict_lower(tkp)
    state_pad = jnp.pad(state_pool, ((0, 0), (0, 0), (POOL_HALO - hist, 0), (0, 0)))
    gfin = row(g_final)

    xp, xs = x_prompt, x_sample
    outs = [[] for _ in range(6)]
    for l in range(depth):
        last = l == depth - 1
        qp, kp, vp, upp = _in_proj(xp, row(g_mix[l]), w_in_b[l], bt=1, ts=ts_prompt, nh=nh, dh=dh)
        attp = _attn(qp, kp, vp, t_prompt, tq=tq, hp=hp)
        xp = _mix(xp, attp, upp, upp, row(g_sb_out[l]), w_pool_b[l], row(pool_scale[l]), w_out_b[l],
                  bt=1, ts=ts_prompt, pos_offset=0, zero_first_halo=True)
        xp = _ffn(xp.reshape(bp * sp_len, d), p_prompt[l].reshape(bp * sp_len, -1),
                  row(g_ffn[l]), wg_b[l], wu_b[l], wd_b[l], row(g_ple[l]), wple_b[l], wpg_b[l], gfin,
                  tm=tm_prompt, final_norm=last).reshape(bp, sp_len, d)
        qs, ks, vs, ups = _in_proj(xs, row(g_mix[l]), w_in_b[l], bt=bs, ts=ss_len, nh=nh, dh=dh)
        atts = _attn_dec(qs, ks, vs, cache_k[l], cache_v[l], t_new, t_past)
        xs = _mix(xs, atts, ups, state_pad[l], row(g_sb_out[l]), w_pool_b[l], row(pool_scale[l]),
                  w_out_b[l], bt=bs, ts=ss_len, pos_offset=past, zero_first_halo=False)
        xs = _ffn(xs.reshape(bs * ss_len, d), p_sample[l].reshape(bs * ss_len, -1),
                  row(g_ffn[l]), wg_b[l], wu_b[l], wd_b[l], row(g_ple[l]), wple_b[l], wpg_b[l], gfin,
                  tm=bs * ss_len, final_norm=last).reshape(bs, ss_len, d)

        new_hist_s = jnp.concatenate([state_pool[l], ups], axis=1)[:, -hist:]
        for lst, val in zip(outs, (kp, vp, upp[:, -hist:], ks, vs, new_hist_s)):
            lst.append(val)

    return (xp, xs) + tuple(jnp.stack(o) for o in outs)


def kernel(x_prompt, x_sample, cache_k, cache_v, state_pool, p_prompt, p_sample, g_mix, w_in, g_sb_out, w_pool, pool_scale, w_out, g_ffn, w_ffn_gate, w_ffn_up, w_ffn_down, g_ple, w_ple, w_ple_gate, g_final):
    return _forward(x_prompt, x_sample, cache_k, cache_v, state_pool, p_prompt, p_sample,
                    g_mix, w_in, g_sb_out, w_pool, pool_scale, w_out,
                    g_ffn, w_ffn_gate, w_ffn_up, w_ffn_down, g_ple, w_ple, w_ple_gate, g_final,
                    ts_prompt=512, tq=256, hp=2, tm_prompt=256, tkp=256)
```

```python
import functools

import jax
import jax.numpy as jnp
from jax import lax
from jax.experimental import pallas as pl
from jax.experimental.pallas import tpu as pltpu

F32 = jnp.float32
BF16 = jnp.bfloat16

EPS = 1e-6
POOL_WINDOWS = (2, 4, 8, 16)
POOL_HALO = 16
VMEM_LIMIT_BYTES = 56 * 1024 * 1024
EXP_UNDERFLOW = 160.0
LOG2E = 1.4426950408889634


def _rms(x, g):
    ms = jnp.mean(x * x, axis=-1, keepdims=True)
    return (x * lax.rsqrt(ms + EPS)) * g


def _layer_spec(arr, l, buffers=None):
    shape = arr.shape[1:]
    index_map = lambda *_: (l,) + (0,) * len(shape)
    if buffers is None:
        return pl.BlockSpec((None,) + shape, index_map)
    return pl.BlockSpec((None,) + shape, index_map, pipeline_mode=pl.Buffered(buffers))


def _const_spec(shape):
    return pl.BlockSpec(shape, lambda *_: (0,) * len(shape))


def _params(sem):
    return pltpu.CompilerParams(dimension_semantics=sem, vmem_limit_bytes=VMEM_LIMIT_BYTES)


def _in_proj_kernel(*refs, bt, ts, nh, dh, qscale, aliased):
    x_ref, g_ref, w_ref = refs[:3]
    q_ref, kb_ref, vb_ref, up_ref, k32_ref, v32_ref = refs[3 + aliased:]
    d = x_ref.shape[-1]
    sbw = nh * dh
    x = x_ref[...].reshape(bt * ts, d)
    h = _rms(x, g_ref[...]).astype(BF16)

    def heads(part):
        u = jnp.dot(h, w_ref[:, part * sbw:(part + 1) * sbw], preferred_element_type=F32)
        for b in range(bt):
            for hh in range(nh):
                yield b, hh, u[b * ts:(b + 1) * ts, hh * dh:(hh + 1) * dh]

    for b, hh, u in heads(0):
        q_ref[b, hh] = (u * qscale).astype(BF16)
    for b, hh, u in heads(1):
        k32_ref[b, hh] = u
        kb_ref[b, hh] = u.astype(BF16)
    for b, hh, u in heads(2):
        v32_ref[b, hh] = u
        vb_ref[b, hh] = u.astype(BF16)
    u = jnp.dot(h, w_ref[:, 3 * sbw:], preferred_element_type=F32)
    for b in range(bt):
        up_ref[b] = u[b * ts:(b + 1) * ts, :]


def _in_proj(x, g, w, kstack, vstack, l, *, bt, ts, nh, dh):
    b, s, d = x.shape
    depth = w.shape[0]
    pw = w.shape[2] - 3 * nh * dh
    aliased = 0 if kstack is None else 2
    hm = jax.ShapeDtypeStruct((b, nh, s, dh), BF16)
    hm_spec = pl.BlockSpec((bt, nh, ts, dh), lambda i, j: (i, 0, j, 0))
    stack = jax.ShapeDtypeStruct((depth, b, nh, s, dh), F32)
    stack_spec = pl.BlockSpec((None, bt, nh, ts, dh), lambda i, j: (l, i, 0, j, 0))
    any_spec = pl.BlockSpec(memory_space=pl.ANY)
    return pl.pallas_call(
        functools.partial(_in_proj_kernel, bt=bt, ts=ts, nh=nh, dh=dh, qscale=LOG2E * dh ** -0.5,
                          aliased=aliased),
        grid=(b // bt, s // ts),
        in_specs=[pl.BlockSpec((bt, ts, d), lambda i, j: (i, j, 0)),
                  _layer_spec(g, l), _layer_spec(w, l)] + [any_spec] * aliased,
        out_specs=[hm_spec, hm_spec, hm_spec,
                   pl.BlockSpec((bt, ts, pw), lambda i, j: (i, j, 0)), stack_spec, stack_spec],
        out_shape=[hm, hm, hm, jax.ShapeDtypeStruct((b, s, pw), F32), stack, stack],
        input_output_aliases={3: 4, 4: 5} if aliased else {},
        compiler_params=_params(("parallel", "parallel")),
        name="in_proj",
    )(x, g, w, *((kstack, vstack) if aliased else ()))


def _strict_lower(n):
    j = lax.broadcasted_iota(jnp.int32, (n, n), 0)
    s = lax.broadcasted_iota(jnp.int32, (n, n), 1)
    return (j > s).astype(BF16)


def _sb_tile(q, kb, vb, tmat, c, acc, mask):
    z = lax.dot_general(q, kb, (((1,), (1,)), ((), ())), preferred_element_type=F32)
    sp = jnp.maximum(z, 0.0) + jnp.log(1.0 + jnp.exp2(-jnp.abs(z))) * LOG2E
    lb = z - sp
    if mask is not None:
        sp = jnp.where(mask, sp, 0.0)
    spb = sp.astype(BF16)
    pex = jnp.dot(spb, tmat, preferred_element_type=F32)
    a = jnp.exp2(lb - pex - c)
    if mask is not None:
        a = jnp.where(mask, a, 0.0)
    acc = acc + jnp.dot(a.astype(BF16), vb, preferred_element_type=F32)
    c = c + pex[:, :1] + spb[:, :1].astype(F32)
    return c, acc


def _causal_mask(n):
    row = lax.broadcasted_iota(jnp.int32, (n, n), 0)
    col = lax.broadcasted_iota(jnp.int32, (n, n), 1)
    return col < row


def _cmin(state):
    return functools.reduce(jnp.minimum, [jnp.min(c) for c, _ in state])


def _attn_kernel(q_ref, k_ref, v_ref, t_ref, o_ref, *, tq, hp, nsub):
    step_i = pl.program_id(2)
    dh = q_ref.shape[-1]
    tmat = t_ref[...]
    mask = _causal_mask(tq)
    chains = [(h, sub) for sub in range(nsub) for h in range(hp)]

    def tile(h, sub, j, c, acc, m=None):
        start = pl.multiple_of(j * tq, tq)
        return _sb_tile(q_ref[0, h, sub * tq:(sub + 1) * tq, :], k_ref[0, h, pl.ds(start, tq), :],
                        v_ref[0, h, pl.ds(start, tq), :], tmat, c, acc, m)

    def diag(h, sub):
        return tile(h, sub, step_i * nsub + sub, jnp.zeros((tq, 1), F32), jnp.zeros((tq, dh), F32),
                    mask)

    def pair(h, sub):
        return tile(h, sub, step_i * nsub + sub - 1, *diag(h, sub))

    state = lax.cond(step_i == 0,
                     lambda: tuple(diag(h, sub) if sub == 0 else pair(h, sub) for h, sub in chains),
                     lambda: tuple(pair(h, sub) for h, sub in chains))

    def cond(st):
        return jnp.logical_and(st[0] >= 0, st[1] < EXP_UNDERFLOW)

    for sub in range(nsub):
        mine = slice(sub * hp, (sub + 1) * hp)

        def body(st, sub=sub):
            j, _, sub_state = st
            sub_state = tuple(tile(h, sub, j, *sub_state[h]) for h in range(hp))
            return j - 1, _cmin(sub_state), sub_state

        first = step_i * nsub + sub - 2
        _, _, sub_state = lax.while_loop(cond, body, (first, _cmin(state[mine]), state[mine]))
        for h in range(hp):
            o_ref[0, h, sub * tq:(sub + 1) * tq, :] = sub_state[h][1]


def _attn(q, k, v, tmat, *, tq, hp, nsub):
    b, nh, s, dh = q.shape
    rows = nsub * tq
    return pl.pallas_call(
        functools.partial(_attn_kernel, tq=tq, hp=hp, nsub=nsub),
        grid=(b, nh // hp, s // rows),
        in_specs=[pl.BlockSpec((1, hp, rows, dh), lambda bi, hi, i: (bi, hi, i, 0)),
                  pl.BlockSpec((1, hp, s, dh), lambda bi, hi, i: (bi, hi, 0, 0)),
                  pl.BlockSpec((1, hp, s, dh), lambda bi, hi, i: (bi, hi, 0, 0)),
                  _const_spec(tmat.shape)],
        out_specs=pl.BlockSpec((1, hp, rows, dh), lambda bi, hi, i: (bi, hi, i, 0)),
        out_shape=jax.ShapeDtypeStruct((b, nh, s, dh), F32),
        compiler_params=_params(("parallel", "parallel", "arbitrary")),
        name="attn",
    )(q, k, v, tmat)


def _attn_dec_kernel(q_ref, kn_ref, vn_ref, kp_ref, vp_ref, tn_ref, tp_ref, o_ref, *, tkp):
    nh, ts, dh = q_ref.shape[-3:]
    past = kp_ref.shape[-2]
    mask = _causal_mask(ts)

    def past_block(j, state):
        rows = slice(j * tkp, (j + 1) * tkp)
        return tuple(_sb_tile(q_ref[0, h], kp_ref[0, h, rows, :].astype(BF16),
                              vp_ref[0, h, rows, :].astype(BF16), tp_ref[...], *state[h], None)
                     for h in range(nh))

    state = tuple(_sb_tile(q_ref[0, h], kn_ref[0, h], vn_ref[0, h], tn_ref[...],
                           jnp.zeros((ts, 1), F32), jnp.zeros((ts, dh), F32), mask)
                  for h in range(nh))
    nblk = past // tkp
    state = past_block(nblk - 1, state)
    for j in reversed(range(nblk - 1)):
        state = lax.cond(_cmin(state) < EXP_UNDERFLOW,
                         functools.partial(past_block, j), lambda st: st, state)
    for h in range(nh):
        o_ref[0, h] = state[h][1]


def _attn_dec(q, kn, vn, kp, vp, tn, tp, l):
    b, nh, ts, dh = q.shape
    past = kp.shape[3]
    new_spec = pl.BlockSpec((1, nh, ts, dh), lambda bi: (bi, 0, 0, 0))
    past_spec = pl.BlockSpec((None, 1, nh, past, dh), lambda bi: (l, bi, 0, 0, 0))
    return pl.pallas_call(
        functools.partial(_attn_dec_kernel, tkp=tp.shape[0]),
        grid=(b,),
        in_specs=[new_spec, new_spec, new_spec, past_spec, past_spec,
                  _const_spec(tn.shape), _const_spec(tp.shape)],
        out_specs=new_spec,
        out_shape=jax.ShapeDtypeStruct((b, nh, ts, dh), F32),
        compiler_params=_params(("parallel",)),
        name="attn_dec",
    )(q, kn, vn, kp, vp, tn, tp)


def _mix_kernel(x_ref, att_ref, up_ref, halo_ref, gsb_ref, wpool_ref, pscale_ref, wout_ref,
                o_ref, ext_ref, *, bt, ts, nh, pos_offset, zero_first_halo):
    j = pl.program_id(1)
    pw = up_ref.shape[-1]
    gd = pw // len(POOL_WINDOWS)
    pos = pos_offset + j * ts + lax.broadcasted_iota(jnp.int32, (ts, 1), 0)
    for b in range(bt):
        att = jnp.concatenate([att_ref[b, hh] for hh in range(nh)], axis=-1)
        att = _rms(att, gsb_ref[...]).astype(BF16)

        halo = halo_ref[b]
        if zero_first_halo:
            halo = jnp.where(j == 0, 0.0, halo)
        up = up_ref[b]
        ext_ref[0:POOL_HALO, :] = halo
        ext_ref[POOL_HALO:, :] = up
        ys = []
        for g, w in enumerate(POOL_WINDOWS):
            cols = slice(g * gd, (g + 1) * gd)
            win = up[:, cols]
            for sh in range(1, w):
                win = win + ext_ref[POOL_HALO - sh:POOL_HALO - sh + ts, cols]
            cnt = jnp.minimum(w, pos + 1).astype(F32)
            dg = win / cnt - up[:, cols]
            ys.append(jnp.dot(dg.astype(BF16), wpool_ref[g], preferred_element_type=F32))
        pool = _rms(jnp.concatenate(ys, axis=-1), pscale_ref[...]).astype(BF16)

        mixed = jnp.concatenate([att, pool], axis=-1)
        o_ref[b] = x_ref[b] + jnp.dot(mixed, wout_ref[...], preferred_element_type=F32)


def _mix(x, att, up, halo_src, gsb, wpool, pscale, wout, l, *, bt, ts, pos_offset, zero_first_halo):
    b, s, d = x.shape
    nh, dh = att.shape[1], att.shape[3]
    pw = up.shape[-1]
    if zero_first_halo:
        halo_spec = pl.BlockSpec((bt, POOL_HALO, pw),
                                 lambda i, j: (i, jnp.maximum(j * (ts // POOL_HALO) - 1, 0), 0))
    else:
        halo_spec = pl.BlockSpec((None, bt, POOL_HALO, pw), lambda i, j: (l, i, 0, 0))
    return pl.pallas_call(
        functools.partial(_mix_kernel, bt=bt, ts=ts, nh=nh, pos_offset=pos_offset,
                          zero_first_halo=zero_first_halo),
        grid=(b // bt, s // ts),
        in_specs=[pl.BlockSpec((bt, ts, d), lambda i, j: (i, j, 0)),
                  pl.BlockSpec((bt, nh, ts, dh), lambda i, j: (i, 0, j, 0)),
                  pl.BlockSpec((bt, ts, pw), lambda i, j: (i, j, 0)),
                  halo_spec,
                  _layer_spec(gsb, l), _layer_spec(wpool, l),
                  _layer_spec(pscale, l), _layer_spec(wout, l)],
        out_specs=pl.BlockSpec((bt, ts, d), lambda i, j: (i, j, 0)),
        out_shape=jax.ShapeDtypeStruct((b, s, d), F32),
        scratch_shapes=[pltpu.VMEM((POOL_HALO + ts, pw), F32)],
        compiler_params=_params(("parallel", "arbitrary")),
        name="mix",
    )(x, att, up, halo_src, gsb, wpool, pscale, wout)


def _ffn_kernel(x_ref, p_ref, gffn_ref, wg_ref, wu_ref, wd_ref, gple_ref, wple_ref, wpg_ref,
                gfin_ref, o_ref, *, final_norm):
    x = x_ref[...]
    h = _rms(x, gffn_ref[...]).astype(BF16)
    gate = jnp.dot(h, wg_ref[...], preferred_element_type=F32)
    upv = jnp.dot(h, wu_ref[...], preferred_element_type=F32)
    hid = (gate * jax.nn.sigmoid(gate) * upv).astype(BF16)
    x = x + jnp.dot(hid, wd_ref[...], preferred_element_type=F32)
    hp = _rms(x, gple_ref[...]).astype(BF16)
    pg = jax.nn.sigmoid(jnp.dot(hp, wpg_ref[...], preferred_element_type=F32))
    x = x + jnp.dot(p_ref[...].astype(BF16), wple_ref[...], preferred_element_type=F32) * pg
    if final_norm:
        x = _rms(x, gfin_ref[...])
    o_ref[...] = x


def _ffn(x, p, gffn, wg, wu, wd, gple, wple, wpg, gfin, l, *, tm, final_norm):
    t, d = x.shape
    pd = p.shape[-1]
    wspec = functools.partial(_layer_spec, l=l, buffers=1)
    return pl.pallas_call(
        functools.partial(_ffn_kernel, final_norm=final_norm),
        grid=(t // tm,),
        in_specs=[pl.BlockSpec((tm, d), lambda i: (i, 0)),
                  pl.BlockSpec((None, tm, pd), lambda i: (l, i, 0)),
                  _layer_spec(gffn, l), wspec(wg), wspec(wu), wspec(wd),
                  _layer_spec(gple, l), wspec(wple), wspec(wpg), _const_spec(gfin.shape)],
        out_specs=pl.BlockSpec((tm, d), lambda i: (i, 0)),
        out_shape=jax.ShapeDtypeStruct((t, d), F32),
        compiler_params=_params(("parallel",)),
        name="ffn",
    )(x, p, gffn, wg, wu, wd, gple, wple, wpg, gfin)


def _forward(x_prompt, x_sample, cache_k, cache_v, state_pool, p_prompt, p_sample,
             g_mix, w_in, g_sb_out, w_pool, pool_scale, w_out,
             g_ffn, w_ffn_gate, w_ffn_up, w_ffn_down, g_ple, w_ple, w_ple_gate, g_final,
             *, ts_prompt, tq, hp, nsub, tm_prompt, tkp):
    depth = w_in.shape[0]
    bp, sp_len, d = x_prompt.shape
    bs, ss_len, _ = x_sample.shape
    nh, past, dh = cache_k.shape[2:]
    hist = state_pool.shape[2]

    rows = lambda a: a.reshape(depth, 1, -1)
    g_mix, g_sb_out, pool_scale, g_ffn, g_ple = map(rows, (g_mix, g_sb_out, pool_scale, g_ffn, g_ple))
    w_in, w_pool, w_out, wg, wu, wd, wple, wpg = (
        a.astype(BF16) for a in (w_in, w_pool, w_out, w_ffn_gate, w_ffn_up, w_ffn_down,
                                 w_ple, w_ple_gate))
    gfin = g_final.reshape(1, -1)
    t_prompt, t_new, t_past = _strict_lower(tq), _strict_lower(ss_len), _strict_lower(tkp)
    state_pad = jnp.pad(state_pool, ((0, 0), (0, 0), (POOL_HALO - hist, 0), (0, 0)))
    pp = p_prompt.reshape(depth, bp * sp_len, -1)
    ps = p_sample.reshape(depth, bs * ss_len, -1)

    xp, xs = x_prompt, x_sample
    kp32 = vp32 = ks32 = vs32 = None
    pool_p, pool_s = [], []
    for l in range(depth):
        last = l == depth - 1
        qp, kp, vp, upp, kp32, vp32 = _in_proj(xp, g_mix, w_in, kp32, vp32, l,
                                               bt=1, ts=ts_prompt, nh=nh, dh=dh)
        attp = _attn(qp, kp, vp, t_prompt, tq=tq, hp=hp, nsub=nsub)
        xp = _mix(xp, attp, upp, upp, g_sb_out, w_pool, pool_scale, w_out, l,
                  bt=1, ts=ts_prompt, pos_offset=0, zero_first_halo=True)
        xp = _ffn(xp.reshape(bp * sp_len, d), pp, g_ffn, wg, wu, wd, g_ple, wple, wpg, gfin, l,
                  tm=tm_prompt, final_norm=last).reshape(bp, sp_len, d)
        qs, ks, vs, ups, ks32, vs32 = _in_proj(xs, g_mix, w_in, ks32, vs32, l,
                                               bt=bs, ts=ss_len, nh=nh, dh=dh)
        atts = _attn_dec(qs, ks, vs, cache_k, cache_v, t_new, t_past, l)
        xs = _mix(xs, atts, ups, state_pad, g_sb_out, w_pool, pool_scale, w_out, l,
                  bt=bs, ts=ss_len, pos_offset=past, zero_first_halo=False)
        xs = _ffn(xs.reshape(bs * ss_len, d), ps, g_ffn, wg, wu, wd, g_ple, wple, wpg, gfin, l,
                  tm=bs * ss_len, final_norm=last).reshape(bs, ss_len, d)

        pool_p.append(upp[:, -hist:])
        pool_s.append(jnp.concatenate([state_pool[l], ups], axis=1)[:, -hist:])

    return xp, xs, kp32, vp32, jnp.stack(pool_p), ks32, vs32, jnp.stack(pool_s)


def kernel(x_prompt, x_sample, cache_k, cache_v, state_pool, p_prompt, p_sample, g_mix, w_in, g_sb_out, w_pool, pool_scale, w_out, g_ffn, w_ffn_gate, w_ffn_up, w_ffn_down, g_ple, w_ple, w_ple_gate, g_final):
    return _forward(x_prompt, x_sample, cache_k, cache_v, state_pool, p_prompt, p_sample,
                    g_mix, w_in, g_sb_out, w_pool, pool_scale, w_out,
                    g_ffn, w_ffn_gate, w_ffn_up, w_ffn_down, g_ple, w_ple, w_ple_gate, g_final,
                    ts_prompt=512, tq=256, hp=4, nsub=2, tm_prompt=512, tkp=256)
```

```python
import functools

import jax
import jax.numpy as jnp
from jax import lax
from jax.experimental import pallas as pl
from jax.experimental.pallas import tpu as pltpu

F32 = jnp.float32
BF16 = jnp.bfloat16

EPS = 1e-6
POOL_WINDOWS = (2, 4, 8, 16)
POOL_HALO = 16
VMEM_LIMIT_BYTES = 56 * 1024 * 1024
EXP_UNDERFLOW = 160.0
LOG2E = 1.4426950408889634


def _rms(x, g):
    ms = jnp.mean(x * x, axis=-1, keepdims=True)
    return (x * lax.rsqrt(ms + EPS)) * g


def _layer_spec(arr, l, buffers=None):
    shape = arr.shape[1:]
    index_map = lambda *_: (l,) + (0,) * len(shape)
    if buffers is None:
        return pl.BlockSpec((None,) + shape, index_map)
    return pl.BlockSpec((None,) + shape, index_map, pipeline_mode=pl.Buffered(buffers))


def _const_spec(shape):
    return pl.BlockSpec(shape, lambda *_: (0,) * len(shape))


def _params(sem):
    return pltpu.CompilerParams(dimension_semantics=sem, vmem_limit_bytes=VMEM_LIMIT_BYTES)


def _in_proj_kernel(*refs, bt, ts, nh, dh, qscale, aliased):
    x_ref, g_ref, w_ref = refs[:3]
    q_ref, kb_ref, vb_ref, up_ref, k32_ref, v32_ref = refs[3 + aliased:]
    d = x_ref.shape[-1]
    sbw = nh * dh
    x = x_ref[...].reshape(bt * ts, d)
    h = _rms(x, g_ref[...]).astype(BF16)

    def heads(part):
        u = jnp.dot(h, w_ref[:, part * sbw:(part + 1) * sbw], preferred_element_type=F32)
        for b in range(bt):
            for hh in range(nh):
                yield b, hh, u[b * ts:(b + 1) * ts, hh * dh:(hh + 1) * dh]

    for b, hh, u in heads(0):
        q_ref[b, hh] = (u * qscale).astype(BF16)
    for b, hh, u in heads(1):
        k32_ref[b, hh] = u
        kb_ref[b, hh] = u.astype(BF16)
    for b, hh, u in heads(2):
        v32_ref[b, hh] = u
        vb_ref[b, hh] = u.astype(BF16)
    u = jnp.dot(h, w_ref[:, 3 * sbw:], preferred_element_type=F32)
    for b in range(bt):
        up_ref[b] = u[b * ts:(b + 1) * ts, :]


def _in_proj(x, g, w, kstack, vstack, l, *, bt, ts, nh, dh):
    b, s, d = x.shape
    depth = w.shape[0]
    pw = w.shape[2] - 3 * nh * dh
    aliased = 0 if kstack is None else 2
    hm = jax.ShapeDtypeStruct((b, nh, s, dh), BF16)
    hm_spec = pl.BlockSpec((bt, nh, ts, dh), lambda i, j: (i, 0, j, 0))
    stack = jax.ShapeDtypeStruct((depth, b, nh, s, dh), F32)
    stack_spec = pl.BlockSpec((None, bt, nh, ts, dh), lambda i, j: (l, i, 0, j, 0))
    any_spec = pl.BlockSpec(memory_space=pl.ANY)
    return pl.pallas_call(
        functools.partial(_in_proj_kernel, bt=bt, ts=ts, nh=nh, dh=dh, qscale=LOG2E * dh ** -0.5,
                          aliased=aliased),
        grid=(b // bt, s // ts),
        in_specs=[pl.BlockSpec((bt, ts, d), lambda i, j: (i, j, 0)),
                  _layer_spec(g, l), _layer_spec(w, l)] + [any_spec] * aliased,
        out_specs=[hm_spec, hm_spec, hm_spec,
                   pl.BlockSpec((bt, ts, pw), lambda i, j: (i, j, 0)), stack_spec, stack_spec],
        out_shape=[hm, hm, hm, jax.ShapeDtypeStruct((b, s, pw), F32), stack, stack],
        input_output_aliases={3: 4, 4: 5} if aliased else {},
        compiler_params=_params(("parallel", "parallel")),
        name="in_proj",
    )(x, g, w, *((kstack, vstack) if aliased else ()))


def _strict_lower(n):
    j = lax.broadcasted_iota(jnp.int32, (n, n), 0)
    s = lax.broadcasted_iota(jnp.int32, (n, n), 1)
    return (j > s).astype(BF16)


def _sb_tile(q, kb, vb, tmat, c, acc, mask, kv_transposed=False):
    nt = (((1,), (1,)), ((), ()))
    if kv_transposed:
        z = jnp.dot(q, kb, preferred_element_type=F32)
    else:
        z = lax.dot_general(q, kb, nt, preferred_element_type=F32)
    sp = jnp.maximum(z, 0.0) + jnp.log(1.0 + jnp.exp2(-jnp.abs(z))) * LOG2E
    lb = z - sp
    if mask is not None:
        sp = jnp.where(mask, sp, 0.0)
    spb = sp.astype(BF16)
    pex = jnp.dot(spb, tmat, preferred_element_type=F32)
    a = jnp.exp2(lb - pex - c)
    if mask is not None:
        a = jnp.where(mask, a, 0.0)
    if kv_transposed:
        acc = acc + lax.dot_general(a.astype(BF16), vb, nt, preferred_element_type=F32)
    else:
        acc = acc + jnp.dot(a.astype(BF16), vb, preferred_element_type=F32)
    c = c + pex[:, :1] + spb[:, :1].astype(F32)
    return c, acc


def _causal_mask(n):
    row = lax.broadcasted_iota(jnp.int32, (n, n), 0)
    col = lax.broadcasted_iota(jnp.int32, (n, n), 1)
    return col < row


def _cmin(state):
    return functools.reduce(jnp.minimum, [jnp.min(c) for c, _ in state])


def _attn_kernel(q_ref, k_ref, v_ref, t_ref, o_ref, *, tq, hp, nsub):
    step_i = pl.program_id(2)
    dh = q_ref.shape[-1]
    tmat = t_ref[...]
    mask = _causal_mask(tq)
    chains = [(h, sub) for sub in range(nsub) for h in range(hp)]

    def tile(h, sub, j, c, acc, m=None):
        start = pl.multiple_of(j * tq, tq)
        return _sb_tile(q_ref[0, h, sub * tq:(sub + 1) * tq, :], k_ref[0, h, pl.ds(start, tq), :],
                        v_ref[0, h, pl.ds(start, tq), :], tmat, c, acc, m)

    def diag(h, sub):
        return tile(h, sub, step_i * nsub + sub, jnp.zeros((tq, 1), F32), jnp.zeros((tq, dh), F32),
                    mask)

    def pair(h, sub):
        return tile(h, sub, step_i * nsub + sub - 1, *diag(h, sub))

    state = lax.cond(step_i == 0,
                     lambda: tuple(diag(h, sub) if sub == 0 else pair(h, sub) for h, sub in chains),
                     lambda: tuple(pair(h, sub) for h, sub in chains))

    def cond(st):
        return jnp.logical_and(st[0] >= 0, st[1] < EXP_UNDERFLOW)

    for sub in range(nsub):
        mine = slice(sub * hp, (sub + 1) * hp)

        def body(st, sub=sub):
            j, _, sub_state = st
            sub_state = tuple(tile(h, sub, j, *sub_state[h]) for h in range(hp))
            return j - 1, _cmin(sub_state), sub_state

        first = step_i * nsub + sub - 2
        _, _, sub_state = lax.while_loop(cond, body, (first, _cmin(state[mine]), state[mine]))
        for h in range(hp):
            o_ref[0, sub * tq:(sub + 1) * tq, h * dh:(h + 1) * dh] = sub_state[h][1]


def _attn(q, k, v, tmat, *, tq, hp, nsub):
    b, nh, s, dh = q.shape
    rows = nsub * tq
    return pl.pallas_call(
        functools.partial(_attn_kernel, tq=tq, hp=hp, nsub=nsub),
        grid=(b, nh // hp, s // rows),
        in_specs=[pl.BlockSpec((1, hp, rows, dh), lambda bi, hi, i: (bi, hi, i, 0)),
                  pl.BlockSpec((1, hp, s, dh), lambda bi, hi, i: (bi, hi, 0, 0)),
                  pl.BlockSpec((1, hp, s, dh), lambda bi, hi, i: (bi, hi, 0, 0)),
                  _const_spec(tmat.shape)],
        out_specs=pl.BlockSpec((1, rows, hp * dh), lambda bi, hi, i: (bi, i, hi)),
        out_shape=jax.ShapeDtypeStruct((b, s, nh * dh), F32),
        compiler_params=_params(("parallel", "parallel", "arbitrary")),
        name="attn",
    )(q, k, v, tmat)


def _attn_dec_kernel(q_ref, kn_ref, vn_ref, kp_ref, vp_ref, tn_ref, tp_ref, o_ref, *, tkp):
    nh, ts, dh = q_ref.shape[-3:]
    past = kp_ref.shape[-1]
    mask = _causal_mask(ts)

    def past_block(j, state):
        cols = slice(j * tkp, (j + 1) * tkp)
        return tuple(_sb_tile(q_ref[0, h], kp_ref[0, h, :, cols].astype(BF16),
                              vp_ref[0, h, :, cols].astype(BF16), tp_ref[...], *state[h], None,
                              kv_transposed=True)
                     for h in range(nh))

    state = tuple(_sb_tile(q_ref[0, h], kn_ref[0, h], vn_ref[0, h], tn_ref[...],
                           jnp.zeros((ts, 1), F32), jnp.zeros((ts, dh), F32), mask)
                  for h in range(nh))
    nblk = past // tkp
    state = past_block(nblk - 1, state)
    for j in reversed(range(nblk - 1)):
        state = lax.cond(_cmin(state) < EXP_UNDERFLOW,
                         functools.partial(past_block, j), lambda st: st, state)
    for h in range(nh):
        o_ref[0, :, h * dh:(h + 1) * dh] = state[h][1]


def _attn_dec(q, kn, vn, kp_t, vp_t, tn, tp, l):
    b, nh, ts, dh = q.shape
    past = kp_t.shape[4]
    new_spec = pl.BlockSpec((1, nh, ts, dh), lambda bi: (bi, 0, 0, 0))
    past_spec = pl.BlockSpec((None, 1, nh, dh, past), lambda bi: (l, bi, 0, 0, 0))
    return pl.pallas_call(
        functools.partial(_attn_dec_kernel, tkp=tp.shape[0]),
        grid=(b,),
        in_specs=[new_spec, new_spec, new_spec, past_spec, past_spec,
                  _const_spec(tn.shape), _const_spec(tp.shape)],
        out_specs=pl.BlockSpec((1, ts, nh * dh), lambda bi: (bi, 0, 0)),
        out_shape=jax.ShapeDtypeStruct((b, ts, nh * dh), F32),
        compiler_params=_params(("parallel",)),
        name="attn_dec",
    )(q, kn, vn, kp_t, vp_t, tn, tp)


def _mix_kernel(x_ref, att_ref, up_ref, halo_ref, gsb_ref, wpool_ref, pscale_ref, wout_ref,
                o_ref, ext_ref, *, bt, ts, pos_offset, zero_first_halo):
    j = pl.program_id(1)
    pw = up_ref.shape[-1]
    gd = pw // len(POOL_WINDOWS)
    pos = pos_offset + j * ts + lax.broadcasted_iota(jnp.int32, (ts, 1), 0)
    for b in range(bt):
        att = _rms(att_ref[b], gsb_ref[...]).astype(BF16)

        halo = halo_ref[b]
        if zero_first_halo:
            halo = jnp.where(j == 0, 0.0, halo)
        up = up_ref[b]
        ext_ref[0:POOL_HALO, :] = halo
        ext_ref[POOL_HALO:, :] = up
        ys = []
        for g, w in enumerate(POOL_WINDOWS):
            cols = slice(g * gd, (g + 1) * gd)
            win = up[:, cols]
            for sh in range(1, w):
                win = win + ext_ref[POOL_HALO - sh:POOL_HALO - sh + ts, cols]
            cnt = jnp.minimum(w, pos + 1).astype(F32)
            dg = win / cnt - up[:, cols]
            ys.append(jnp.dot(dg.astype(BF16), wpool_ref[g], preferred_element_type=F32))
        pool = _rms(jnp.concatenate(ys, axis=-1), pscale_ref[...]).astype(BF16)

        sbw = att.shape[-1]
        o_ref[b] = (x_ref[b] + jnp.dot(att, wout_ref[:sbw, :], preferred_element_type=F32)
                    + jnp.dot(pool, wout_ref[sbw:, :], preferred_element_type=F32))


def _mix(x, att, up, halo_src, gsb, wpool, pscale, wout, l, *, bt, ts, pos_offset, zero_first_halo):
    b, s, d = x.shape
    sbw = att.shape[-1]
    pw = up.shape[-1]
    if zero_first_halo:
        halo_spec = pl.BlockSpec((bt, POOL_HALO, pw),
                                 lambda i, j: (i, jnp.maximum(j * (ts // POOL_HALO) - 1, 0), 0))
    else:
        halo_spec = pl.BlockSpec((None, bt, POOL_HALO, pw), lambda i, j: (l, i, 0, 0))
    return pl.pallas_call(
        functools.partial(_mix_kernel, bt=bt, ts=ts, pos_offset=pos_offset,
                          zero_first_halo=zero_first_halo),
        grid=(b // bt, s // ts),
        in_specs=[pl.BlockSpec((bt, ts, d), lambda i, j: (i, j, 0)),
                  pl.BlockSpec((bt, ts, sbw), lambda i, j: (i, j, 0)),
                  pl.BlockSpec((bt, ts, pw), lambda i, j: (i, j, 0)),
                  halo_spec,
                  _layer_spec(gsb, l), _layer_spec(wpool, l),
                  _layer_spec(pscale, l), _layer_spec(wout, l)],
        out_specs=pl.BlockSpec((bt, ts, d), lambda i, j: (i, j, 0)),
        out_shape=jax.ShapeDtypeStruct((b, s, d), F32),
        scratch_shapes=[pltpu.VMEM((POOL_HALO + ts, pw), F32)],
        compiler_params=_params(("parallel", "arbitrary")),
        name="mix",
    )(x, att, up, halo_src, gsb, wpool, pscale, wout)


def _ffn_kernel(x_ref, p_ref, gffn_ref, wg_ref, wu_ref, wd_ref, gple_ref, wple_ref, wpg_ref,
                gfin_ref, o_ref, *, final_norm):
    x = x_ref[...]
    h = _rms(x, gffn_ref[...]).astype(BF16)
    gate = jnp.dot(h, wg_ref[...], preferred_element_type=F32)
    upv = jnp.dot(h, wu_ref[...], preferred_element_type=F32)
    hid = (gate * jax.nn.sigmoid(gate) * upv).astype(BF16)
    x = x + jnp.dot(hid, wd_ref[...], preferred_element_type=F32)
    hp = _rms(x, gple_ref[...]).astype(BF16)
    pg = jax.nn.sigmoid(jnp.dot(hp, wpg_ref[...], preferred_element_type=F32))
    x = x + jnp.dot(p_ref[...].astype(BF16), wple_ref[...], preferred_element_type=F32) * pg
    if final_norm:
        x = _rms(x, gfin_ref[...])
    o_ref[...] = x


def _ffn(x, p, gffn, wg, wu, wd, gple, wple, wpg, gfin, l, *, tm, final_norm):
    t, d = x.shape
    pd = p.shape[-1]
    wspec = functools.partial(_layer_spec, l=l, buffers=1)
    return pl.pallas_call(
        functools.partial(_ffn_kernel, final_norm=final_norm),
        grid=(t // tm,),
        in_specs=[pl.BlockSpec((tm, d), lambda i: (i, 0)),
                  pl.BlockSpec((None, tm, pd), lambda i: (l, i, 0)),
                  _layer_spec(gffn, l), wspec(wg), wspec(wu), wspec(wd),
                  _layer_spec(gple, l), wspec(wple), wspec(wpg), _const_spec(gfin.shape)],
        out_specs=pl.BlockSpec((tm, d), lambda i: (i, 0)),
        out_shape=jax.ShapeDtypeStruct((t, d), F32),
        compiler_params=_params(("parallel",)),
        name="ffn",
    )(x, p, gffn, wg, wu, wd, gple, wple, wpg, gfin)


def _forward(x_prompt, x_sample, cache_k, cache_v, state_pool, p_prompt, p_sample,
             g_mix, w_in, g_sb_out, w_pool, pool_scale, w_out,
             g_ffn, w_ffn_gate, w_ffn_up, w_ffn_down, g_ple, w_ple, w_ple_gate, g_final,
             *, ts_prompt, tq, hp, nsub, tm_prompt, tkp):
    depth = w_in.shape[0]
    bp, sp_len, d = x_prompt.shape
    bs, ss_len, _ = x_sample.shape
    nh, past, dh = cache_k.shape[2:]
    hist = state_pool.shape[2]

    rows = lambda a: a.reshape(depth, 1, -1)
    g_mix, g_sb_out, pool_scale, g_ffn, g_ple = map(rows, (g_mix, g_sb_out, pool_scale, g_ffn, g_ple))
    w_in, w_pool, w_out, wg, wu, wd, wple, wpg = (
        a.astype(BF16) for a in (w_in, w_pool, w_out, w_ffn_gate, w_ffn_up, w_ffn_down,
                                 w_ple, w_ple_gate))
    gfin = g_final.reshape(1, -1)
    t_prompt, t_new, t_past = _strict_lower(tq), _strict_lower(ss_len), _strict_lower(tkp)
    state_pad = jnp.pad(state_pool, ((0, 0), (0, 0), (POOL_HALO - hist, 0), (0, 0)))
    cache_kt, cache_vt = jnp.swapaxes(cache_k, 3, 4), jnp.swapaxes(cache_v, 3, 4)
    pp = p_prompt.reshape(depth, bp * sp_len, -1)
    ps = p_sample.reshape(depth, bs * ss_len, -1)

    xp, xs = x_prompt, x_sample
    kp32 = vp32 = ks32 = vs32 = None
    pool_p, pool_s = [], []
    for l in range(depth):
        last = l == depth - 1
        qp, kp, vp, upp, kp32, vp32 = _in_proj(xp, g_mix, w_in, kp32, vp32, l,
                                               bt=1, ts=ts_prompt, nh=nh, dh=dh)
        attp = _attn(qp, kp, vp, t_prompt, tq=tq, hp=hp, nsub=nsub)
        xp = _mix(xp, attp, upp, upp, g_sb_out, w_pool, pool_scale, w_out, l,
                  bt=1, ts=ts_prompt, pos_offset=0, zero_first_halo=True)
        xp = _ffn(xp.reshape(bp * sp_len, d), pp, g_ffn, wg, wu, wd, g_ple, wple, wpg, gfin, l,
                  tm=tm_prompt, final_norm=last).reshape(bp, sp_len, d)
        qs, ks, vs, ups, ks32, vs32 = _in_proj(xs, g_mix, w_in, ks32, vs32, l,
                                               bt=bs, ts=ss_len, nh=nh, dh=dh)
        atts = _attn_dec(qs, ks, vs, cache_kt, cache_vt, t_new, t_past, l)
        xs = _mix(xs, atts, ups, state_pad, g_sb_out, w_pool, pool_scale, w_out, l,
                  bt=bs, ts=ss_len, pos_offset=past, zero_first_halo=False)
        xs = _ffn(xs.reshape(bs * ss_len, d), ps, g_ffn, wg, wu, wd, g_ple, wple, wpg, gfin, l,
                  tm=bs * ss_len, final_norm=last).reshape(bs, ss_len, d)

        pool_p.append(upp[:, -hist:])
        pool_s.append(jnp.concatenate([state_pool[l], ups], axis=1)[:, -hist:])

    return xp, xs, kp32, vp32, jnp.stack(pool_p), ks32, vs32, jnp.stack(pool_s)


def kernel(x_prompt, x_sample, cache_k, cache_v, state_pool, p_prompt, p_sample, g_mix, w_in, g_sb_out, w_pool, pool_scale, w_out, g_ffn, w_ffn_gate, w_ffn_up, w_ffn_down, g_ple, w_ple, w_ple_gate, g_final):
    return _forward(x_prompt, x_sample, cache_k, cache_v, state_pool, p_prompt, p_sample,
                    g_mix, w_in, g_sb_out, w_pool, pool_scale, w_out,
                    g_ffn, w_ffn_gate, w_ffn_up, w_ffn_down, g_ple, w_ple, w_ple_gate, g_final,
                    ts_prompt=512, tq=256, hp=4, nsub=2, tm_prompt=512, tkp=256)
```

```python
import functools

import jax
import jax.numpy as jnp
from jax import lax
from jax.experimental import pallas as pl
from jax.experimental.pallas import tpu as pltpu

F32 = jnp.float32
BF16 = jnp.bfloat16

EPS = 1e-6
POOL_WINDOWS = (2, 4, 8, 16)
POOL_HALO = 16
VMEM_LIMIT_BYTES = 56 * 1024 * 1024
EXP_UNDERFLOW = 160.0
LOG2E = 1.4426950408889634
SOFTPLUS_LINEAR = 64.0
MASKED_LOGIT = -1e30


def _rms(x, g):
    ms = jnp.mean(x * x, axis=-1, keepdims=True)
    return (x * lax.rsqrt(ms + EPS)) * g


def _layer_spec(arr, l, buffers=None):
    shape = arr.shape[1:]
    index_map = lambda *_: (l,) + (0,) * len(shape)
    if buffers is None:
        return pl.BlockSpec((None,) + shape, index_map)
    return pl.BlockSpec((None,) + shape, index_map, pipeline_mode=pl.Buffered(buffers))


def _const_spec(shape):
    return pl.BlockSpec(shape, lambda *_: (0,) * len(shape))


def _params(sem):
    return pltpu.CompilerParams(dimension_semantics=sem, vmem_limit_bytes=VMEM_LIMIT_BYTES)


def _in_proj_kernel(*refs, bt, ts, nh, dh, qscale, aliased):
    x_ref, g_ref, w_ref = refs[:3]
    q_ref, kb_ref, vb_ref, up_ref, k32_ref, v32_ref = refs[3 + aliased:]
    d = x_ref.shape[-1]
    sbw = nh * dh
    x = x_ref[...].reshape(bt * ts, d)
    h = _rms(x, g_ref[...]).astype(BF16)

    def heads(part):
        u = jnp.dot(h, w_ref[:, part * sbw:(part + 1) * sbw], preferred_element_type=F32)
        for b in range(bt):
            for hh in range(nh):
                yield b, hh, u[b * ts:(b + 1) * ts, hh * dh:(hh + 1) * dh]

    for b, hh, u in heads(0):
        q_ref[b, hh] = (u * qscale).astype(BF16)
    for b, hh, u in heads(1):
        k32_ref[b, hh] = u
        kb_ref[b, hh] = u.astype(BF16)
    for b, hh, u in heads(2):
        v32_ref[b, hh] = u
        vb_ref[b, hh] = u.astype(BF16)
    u = jnp.dot(h, w_ref[:, 3 * sbw:], preferred_element_type=F32)
    for b in range(bt):
        up_ref[b] = u[b * ts:(b + 1) * ts, :]


def _in_proj(x, g, w, kstack, vstack, l, *, bt, ts, nh, dh):
    b, s, d = x.shape
    depth = w.shape[0]
    pw = w.shape[2] - 3 * nh * dh
    aliased = 0 if kstack is None else 2
    hm = jax.ShapeDtypeStruct((b, nh, s, dh), BF16)
    hm_spec = pl.BlockSpec((bt, nh, ts, dh), lambda i, j: (i, 0, j, 0))
    stack = jax.ShapeDtypeStruct((depth, b, nh, s, dh), F32)
    stack_spec = pl.BlockSpec((None, bt, nh, ts, dh), lambda i, j: (l, i, 0, j, 0))
    any_spec = pl.BlockSpec(memory_space=pl.ANY)
    return pl.pallas_call(
        functools.partial(_in_proj_kernel, bt=bt, ts=ts, nh=nh, dh=dh, qscale=LOG2E * dh ** -0.5,
                          aliased=aliased),
        grid=(b // bt, s // ts),
        in_specs=[pl.BlockSpec((bt, ts, d), lambda i, j: (i, j, 0)),
                  _layer_spec(g, l), _layer_spec(w, l)] + [any_spec] * aliased,
        out_specs=[hm_spec, hm_spec, hm_spec,
                   pl.BlockSpec((bt, ts, pw), lambda i, j: (i, j, 0)), stack_spec, stack_spec],
        out_shape=[hm, hm, hm, jax.ShapeDtypeStruct((b, s, pw), F32), stack, stack],
        input_output_aliases={3: 4, 4: 5} if aliased else {},
        compiler_params=_params(("parallel", "parallel")),
        name="in_proj",
    )(x, g, w, *((kstack, vstack) if aliased else ()))


def _strict_lower(n):
    j = lax.broadcasted_iota(jnp.int32, (n, n), 0)
    s = lax.broadcasted_iota(jnp.int32, (n, n), 1)
    return (j > s).astype(BF16)


NT_DIMS = (((1,), (1,)), ((), ()))


def _softplus2(z):
    return jnp.where(z > SOFTPLUS_LINEAR, z, jnp.log(1.0 + jnp.exp2(z)) * LOG2E)


def _sb_tile(q, kb, vb, tmat, c, acc, mask, kv_transposed=False):
    nt = NT_DIMS
    if kv_transposed:
        z = jnp.dot(q, kb, preferred_element_type=F32)
    else:
        z = lax.dot_general(q, kb, nt, preferred_element_type=F32)
    if mask is not None:
        z = jnp.where(mask, z, MASKED_LOGIT)
    sp = _softplus2(z)
    lb = z - sp
    spb = sp.astype(BF16)
    pex = jnp.dot(spb, tmat, preferred_element_type=F32)
    a = jnp.exp2(lb - pex - c)
    if kv_transposed:
        acc = acc + lax.dot_general(a.astype(BF16), vb, nt, preferred_element_type=F32)
    else:
        acc = acc + jnp.dot(a.astype(BF16), vb, preferred_element_type=F32)
    c = c + pex[:, :1] + spb[:, :1].astype(F32)
    return c, acc


def _causal_mask(n):
    row = lax.broadcasted_iota(jnp.int32, (n, n), 0)
    col = lax.broadcasted_iota(jnp.int32, (n, n), 1)
    return col < row


def _cmin(state):
    return functools.reduce(jnp.minimum, [jnp.min(c) for c, _ in state])


def _attn_kernel(q_ref, k_ref, v_ref, t_ref, o_ref, *, tq, hp, nsub):
    step_i = pl.program_id(2)
    dh = q_ref.shape[-1]
    tmat = t_ref[...]
    mask = _causal_mask(tq)
    chains = [(h, sub) for sub in range(nsub) for h in range(hp)]

    def tile(h, sub, j, c, acc, m=None):
        start = pl.multiple_of(j * tq, tq)
        return _sb_tile(q_ref[0, h, sub * tq:(sub + 1) * tq, :], k_ref[0, h, pl.ds(start, tq), :],
                        v_ref[0, h, pl.ds(start, tq), :], tmat, c, acc, m)

    def diag(h, sub):
        return tile(h, sub, step_i * nsub + sub, jnp.zeros((tq, 1), F32), jnp.zeros((tq, dh), F32),
                    mask)

    def pair(h, sub):
        return tile(h, sub, step_i * nsub + sub - 1, *diag(h, sub))

    def logits(h, sub):
        start = pl.multiple_of((step_i * nsub + sub - 1) * tq, tq)
        z = lax.dot_general(q_ref[0, h, sub * tq:(sub + 1) * tq, :],
                            k_ref[0, h, pl.ds(start, 2 * tq), :], NT_DIMS,
                            preferred_element_type=F32)
        z = jnp.concatenate([z[:, :tq], jnp.where(mask, z[:, tq:], MASKED_LOGIT)], axis=1)
        sp = _softplus2(z)
        return h, start, z - sp, sp[:, :tq].astype(BF16), sp[:, tq:].astype(BF16)

    def suffix_sums(h, start, lb, spb_p, spb_d):
        pex_p = jnp.dot(spb_p, tmat, preferred_element_type=F32)
        pex_d = jnp.dot(spb_d, tmat, preferred_element_type=F32)
        c_d = pex_d[:, :1] + spb_d[:, :1].astype(F32)
        c_p = pex_p[:, :1] + spb_p[:, :1].astype(F32)
        return h, start, lb, pex_p, pex_d, c_d, c_p

    def weights(h, start, lb, pex_p, pex_d, c_d, c_p):
        a_p = jnp.exp2(lb[:, :tq] - pex_p - c_d)
        a_d = jnp.exp2(lb[:, tq:] - pex_d)
        return h, start, jnp.concatenate([a_p, a_d], axis=1).astype(BF16), c_d + c_p

    def output(h, start, a, c):
        return c, jnp.dot(a, v_ref[0, h, pl.ds(start, 2 * tq), :], preferred_element_type=F32)

    def pairs_by_stage():
        stages = (logits, suffix_sums, weights, output)
        vals = [(h, sub) for h, sub in chains]
        for t in range(len(chains) + len(stages) - 1):
            for s in reversed(range(len(stages))):
                n = t - s
                if 0 <= n < len(chains):
                    vals[n] = stages[s](*vals[n])
        return tuple(vals)

    state = lax.cond(step_i == 0,
                     lambda: tuple(diag(h, sub) if sub == 0 else pair(h, sub) for h, sub in chains),
                     pairs_by_stage)

    def cond(st):
        return jnp.logical_and(st[0] >= 0, st[1] < EXP_UNDERFLOW)

    for sub in range(nsub):
        mine = slice(sub * hp, (sub + 1) * hp)

        def body(st, sub=sub):
            j, _, sub_state = st
            sub_state = tuple(tile(h, sub, j, *sub_state[h]) for h in range(hp))
            return j - 1, _cmin(sub_state), sub_state

        first = step_i * nsub + sub - 2
        _, _, sub_state = lax.while_loop(cond, body, (first, _cmin(state[mine]), state[mine]))
        for h in range(hp):
            o_ref[0, sub * tq:(sub + 1) * tq, h * dh:(h + 1) * dh] = sub_state[h][1]


def _attn(q, k, v, tmat, *, tq, hp, nsub):
    b, nh, s, dh = q.shape
    rows = nsub * tq
    return pl.pallas_call(
        functools.partial(_attn_kernel, tq=tq, hp=hp, nsub=nsub),
        grid=(b, nh // hp, s // rows),
        in_specs=[pl.BlockSpec((1, hp, rows, dh), lambda bi, hi, i: (bi, hi, i, 0)),
                  pl.BlockSpec((1, hp, s, dh), lambda bi, hi, i: (bi, hi, 0, 0)),
                  pl.BlockSpec((1, hp, s, dh), lambda bi, hi, i: (bi, hi, 0, 0)),
                  _const_spec(tmat.shape)],
        out_specs=pl.BlockSpec((1, rows, hp * dh), lambda bi, hi, i: (bi, i, hi)),
        out_shape=jax.ShapeDtypeStruct((b, s, nh * dh), F32),
        compiler_params=_params(("parallel", "parallel", "arbitrary")),
        name="attn",
    )(q, k, v, tmat)


def _attn_dec_kernel(q_ref, kn_ref, vn_ref, kp_ref, vp_ref, tn_ref, tp_ref, o_ref, *, tkp):
    nh, ts, dh = q_ref.shape[-3:]
    past = kp_ref.shape[-1]
    mask = _causal_mask(ts)

    def past_block(j, state):
        cols = slice(j * tkp, (j + 1) * tkp)
        return tuple(_sb_tile(q_ref[0, h], kp_ref[0, h, :, cols].astype(BF16),
                              vp_ref[0, h, :, cols].astype(BF16), tp_ref[...], *state[h], None,
                              kv_transposed=True)
                     for h in range(nh))

    state = tuple(_sb_tile(q_ref[0, h], kn_ref[0, h], vn_ref[0, h], tn_ref[...],
                           jnp.zeros((ts, 1), F32), jnp.zeros((ts, dh), F32), mask)
                  for h in range(nh))
    nblk = past // tkp
    state = past_block(nblk - 1, state)
    for j in reversed(range(nblk - 1)):
        state = lax.cond(_cmin(state) < EXP_UNDERFLOW,
                         functools.partial(past_block, j), lambda st: st, state)
    for h in range(nh):
        o_ref[0, :, h * dh:(h + 1) * dh] = state[h][1]


def _attn_dec(q, kn, vn, kp_t, vp_t, tn, tp, l):
    b, nh, ts, dh = q.shape
    past = kp_t.shape[4]
    new_spec = pl.BlockSpec((1, nh, ts, dh), lambda bi: (bi, 0, 0, 0))
    past_spec = pl.BlockSpec((None, 1, nh, dh, past), lambda bi: (l, bi, 0, 0, 0))
    return pl.pallas_call(
        functools.partial(_attn_dec_kernel, tkp=tp.shape[0]),
        grid=(b,),
        in_specs=[new_spec, new_spec, new_spec, past_spec, past_spec,
                  _const_spec(tn.shape), _const_spec(tp.shape)],
        out_specs=pl.BlockSpec((1, ts, nh * dh), lambda bi: (bi, 0, 0)),
        out_shape=jax.ShapeDtypeStruct((b, ts, nh * dh), F32),
        compiler_params=_params(("parallel",)),
        name="attn_dec",
    )(q, kn, vn, kp_t, vp_t, tn, tp)


def _mix_kernel(x_ref, att_ref, up_ref, halo_ref, gsb_ref, wpool_ref, pscale_ref, wout_ref,
                o_ref, ext_ref, *, bt, ts, pos_offset, zero_first_halo):
    j = pl.program_id(1)
    pw = up_ref.shape[-1]
    gd = pw // len(POOL_WINDOWS)
    pos = pos_offset + j * ts + lax.broadcasted_iota(jnp.int32, (ts, 1), 0)
    for b in range(bt):
        att = _rms(att_ref[b], gsb_ref[...]).astype(BF16)

        halo = halo_ref[b]
        if zero_first_halo:
            halo = jnp.where(j == 0, 0.0, halo)
        up = up_ref[b]
        ext_ref[0:POOL_HALO, :] = halo
        ext_ref[POOL_HALO:, :] = up
        ys = []
        for g, w in enumerate(POOL_WINDOWS):
            cols = slice(g * gd, (g + 1) * gd)
            win = up[:, cols]
            for sh in range(1, w):
                win = win + ext_ref[POOL_HALO - sh:POOL_HALO - sh + ts, cols]
            cnt = jnp.minimum(w, pos + 1).astype(F32)
            dg = win / cnt - up[:, cols]
            ys.append(jnp.dot(dg.astype(BF16), wpool_ref[g], preferred_element_type=F32))
        pool = _rms(jnp.concatenate(ys, axis=-1), pscale_ref[...]).astype(BF16)

        sbw = att.shape[-1]
        o_ref[b] = (x_ref[b] + jnp.dot(att, wout_ref[:sbw, :], preferred_element_type=F32)
                    + jnp.dot(pool, wout_ref[sbw:, :], preferred_element_type=F32))


def _mix(x, att, up, halo_src, gsb, wpool, pscale, wout, l, *, bt, ts, pos_offset, zero_first_halo):
    b, s, d = x.shape
    sbw = att.shape[-1]
    pw = up.shape[-1]
    if zero_first_halo:
        halo_spec = pl.BlockSpec((bt, POOL_HALO, pw),
                                 lambda i, j: (i, jnp.maximum(j * (ts // POOL_HALO) - 1, 0), 0))
    else:
        halo_spec = pl.BlockSpec((None, bt, POOL_HALO, pw), lambda i, j: (l, i, 0, 0))
    return pl.pallas_call(
        functools.partial(_mix_kernel, bt=bt, ts=ts, pos_offset=pos_offset,
                          zero_first_halo=zero_first_halo),
        grid=(b // bt, s // ts),
        in_specs=[pl.BlockSpec((bt, ts, d), lambda i, j: (i, j, 0)),
                  pl.BlockSpec((bt, ts, sbw), lambda i, j: (i, j, 0)),
                  pl.BlockSpec((bt, ts, pw), lambda i, j: (i, j, 0)),
                  halo_spec,
                  _layer_spec(gsb, l), _layer_spec(wpool, l),
                  _layer_spec(pscale, l), _layer_spec(wout, l)],
        out_specs=pl.BlockSpec((bt, ts, d), lambda i, j: (i, j, 0)),
        out_shape=jax.ShapeDtypeStruct((b, s, d), F32),
        scratch_shapes=[pltpu.VMEM((POOL_HALO + ts, pw), F32)],
        compiler_params=_params(("parallel", "arbitrary")),
        name="mix",
    )(x, att, up, halo_src, gsb, wpool, pscale, wout)


def _ffn_kernel(x_ref, p_ref, gffn_ref, wg_ref, wu_ref, wd_ref, gple_ref, wple_ref, wpg_ref,
                gfin_ref, o_ref, *, final_norm):
    x = x_ref[...]
    h = _rms(x, gffn_ref[...]).astype(BF16)
    gate = jnp.dot(h, wg_ref[...], preferred_element_type=F32)
    upv = jnp.dot(h, wu_ref[...], preferred_element_type=F32)
    hid = (gate * jax.nn.sigmoid(gate) * upv).astype(BF16)
    x = x + jnp.dot(hid, wd_ref[...], preferred_element_type=F32)
    hp = _rms(x, gple_ref[...]).astype(BF16)
    pg = jax.nn.sigmoid(jnp.dot(hp, wpg_ref[...], preferred_element_type=F32))
    x = x + jnp.dot(p_ref[...].astype(BF16), wple_ref[...], preferred_element_type=F32) * pg
    if final_norm:
        x = _rms(x, gfin_ref[...])
    o_ref[...] = x


def _ffn(x, p, gffn, wg, wu, wd, gple, wple, wpg, gfin, l, *, tm, final_norm):
    t, d = x.shape
    pd = p.shape[-1]
    wspec = functools.partial(_layer_spec, l=l, buffers=1)
    return pl.pallas_call(
        functools.partial(_ffn_kernel, final_norm=final_norm),
        grid=(t // tm,),
        in_specs=[pl.BlockSpec((tm, d), lambda i: (i, 0)),
                  pl.BlockSpec((None, tm, pd), lambda i: (l, i, 0)),
                  _layer_spec(gffn, l), wspec(wg), wspec(wu), wspec(wd),
                  _layer_spec(gple, l), wspec(wple), wspec(wpg), _const_spec(gfin.shape)],
        out_specs=pl.BlockSpec((tm, d), lambda i: (i, 0)),
        out_shape=jax.ShapeDtypeStruct((t, d), F32),
        compiler_params=_params(("parallel",)),
        name="ffn",
    )(x, p, gffn, wg, wu, wd, gple, wple, wpg, gfin)


def _forward(x_prompt, x_sample, cache_k, cache_v, state_pool, p_prompt, p_sample,
             g_mix, w_in, g_sb_out, w_pool, pool_scale, w_out,
             g_ffn, w_ffn_gate, w_ffn_up, w_ffn_down, g_ple, w_ple, w_ple_gate, g_final,
             *, ts_prompt, tq, hp, nsub, tm_prompt, tkp):
    depth = w_in.shape[0]
    bp, sp_len, d = x_prompt.shape
    bs, ss_len, _ = x_sample.shape
    nh, past, dh = cache_k.shape[2:]
    hist = state_pool.shape[2]

    rows = lambda a: a.reshape(depth, 1, -1)
    g_mix, g_sb_out, pool_scale, g_ffn, g_ple = map(rows, (g_mix, g_sb_out, pool_scale, g_ffn, g_ple))
    w_in, w_pool, w_out, wg, wu, wd, wple, wpg = (
        a.astype(BF16) for a in (w_in, w_pool, w_out, w_ffn_gate, w_ffn_up, w_ffn_down,
                                 w_ple, w_ple_gate))
    gfin = g_final.reshape(1, -1)
    t_prompt, t_new, t_past = _strict_lower(tq), _strict_lower(ss_len), _strict_lower(tkp)
    state_pad = jnp.pad(state_pool, ((0, 0), (0, 0), (POOL_HALO - hist, 0), (0, 0)))
    cache_kt, cache_vt = jnp.swapaxes(cache_k, 3, 4), jnp.swapaxes(cache_v, 3, 4)
    pp = p_prompt.reshape(depth, bp * sp_len, -1)
    ps = p_sample.reshape(depth, bs * ss_len, -1)

    xp, xs = x_prompt, x_sample
    kp32 = vp32 = ks32 = vs32 = None
    pool_p, pool_s = [], []
    for l in range(depth):
        last = l == depth - 1
        qp, kp, vp, upp, kp32, vp32 = _in_proj(xp, g_mix, w_in, kp32, vp32, l,
                                               bt=1, ts=ts_prompt, nh=nh, dh=dh)
        attp = _attn(qp, kp, vp, t_prompt, tq=tq, hp=hp, nsub=nsub)
        xp = _mix(xp, attp, upp, upp, g_sb_out, w_pool, pool_scale, w_out, l,
                  bt=1, ts=ts_prompt, pos_offset=0, zero_first_halo=True)
        xp = _ffn(xp.reshape(bp * sp_len, d), pp, g_ffn, wg, wu, wd, g_ple, wple, wpg, gfin, l,
                  tm=tm_prompt, final_norm=last).reshape(bp, sp_len, d)
        qs, ks, vs, ups, ks32, vs32 = _in_proj(xs, g_mix, w_in, ks32, vs32, l,
                                               bt=bs, ts=ss_len, nh=nh, dh=dh)
        atts = _attn_dec(qs, ks, vs, cache_kt, cache_vt, t_new, t_past, l)
        xs = _mix(xs, atts, ups, state_pad, g_sb_out, w_pool, pool_scale, w_out, l,
                  bt=bs, ts=ss_len, pos_offset=past, zero_first_halo=False)
        xs = _ffn(xs.reshape(bs * ss_len, d), ps, g_ffn, wg, wu, wd, g_ple, wple, wpg, gfin, l,
                  tm=bs * ss_len, final_norm=last).reshape(bs, ss_len, d)

        pool_p.append(upp[:, -hist:])
        pool_s.append(jnp.concatenate([state_pool[l], ups], axis=1)[:, -hist:])

    return xp, xs, kp32, vp32, jnp.stack(pool_p), ks32, vs32, jnp.stack(pool_s)


def kernel(x_prompt, x_sample, cache_k, cache_v, state_pool, p_prompt, p_sample, g_mix, w_in, g_sb_out, w_pool, pool_scale, w_out, g_ffn, w_ffn_gate, w_ffn_up, w_ffn_down, g_ple, w_ple, w_ple_gate, g_final):
    return _forward(x_prompt, x_sample, cache_k, cache_v, state_pool, p_prompt, p_sample,
                    g_mix, w_in, g_sb_out, w_pool, pool_scale, w_out,
                    g_ffn, w_ffn_gate, w_ffn_up, w_ffn_down, g_ple, w_ple, w_ple_gate, g_final,
                    ts_prompt=512, tq=256, hp=4, nsub=2, tm_prompt=512, tkp=256)
```

```python
import functools

import jax
import jax.numpy as jnp
from jax import lax
from jax.experimental import pallas as pl
from jax.experimental.pallas import tpu as pltpu

F32 = jnp.float32
BF16 = jnp.bfloat16

EPS = 1e-6
POOL_WINDOWS = (2, 4, 8, 16)
POOL_HALO = 16
VMEM_LIMIT_BYTES = 56 * 1024 * 1024
EXP_UNDERFLOW = 160.0
LOG2E = 1.4426950408889634
SOFTPLUS_LINEAR = 64.0
MASKED_LOGIT = -1e30


def _rms(x, g):
    ms = jnp.mean(x * x, axis=-1, keepdims=True)
    return (x * lax.rsqrt(ms + EPS)) * g


def _layer_spec(arr, l, buffers=None):
    shape = arr.shape[1:]
    index_map = lambda *_: (l,) + (0,) * len(shape)
    if buffers is None:
        return pl.BlockSpec((None,) + shape, index_map)
    return pl.BlockSpec((None,) + shape, index_map, pipeline_mode=pl.Buffered(buffers))


def _const_spec(shape):
    return pl.BlockSpec(shape, lambda *_: (0,) * len(shape))


def _params(sem):
    return pltpu.CompilerParams(dimension_semantics=sem, vmem_limit_bytes=VMEM_LIMIT_BYTES)


def _in_proj_kernel(*refs, bt, ts, nh, dh, qscale, aliased):
    x_ref, g_ref, w_ref = refs[:3]
    q_ref, kb_ref, vb_ref, up_ref, k32_ref, v32_ref = refs[3 + aliased:]
    d = x_ref.shape[-1]
    sbw = nh * dh
    x = x_ref[...].reshape(bt * ts, d)
    h = _rms(x, g_ref[...]).astype(BF16)

    def heads(part):
        u = jnp.dot(h, w_ref[:, part * sbw:(part + 1) * sbw], preferred_element_type=F32)
        for b in range(bt):
            for hh in range(nh):
                yield b, hh, u[b * ts:(b + 1) * ts, hh * dh:(hh + 1) * dh]

    for b, hh, u in heads(0):
        q_ref[b, hh] = (u * qscale).astype(BF16)
    for b, hh, u in heads(1):
        k32_ref[b, hh] = u
        kb_ref[b, hh] = u.astype(BF16)
    for b, hh, u in heads(2):
        v32_ref[b, hh] = u
        vb_ref[b, hh] = u.astype(BF16)
    u = jnp.dot(h, w_ref[:, 3 * sbw:], preferred_element_type=F32)
    for b in range(bt):
        up_ref[b] = u[b * ts:(b + 1) * ts, :]


def _in_proj(x, g, w, kstack, vstack, l, *, bt, ts, nh, dh):
    b, s, d = x.shape
    depth = w.shape[0]
    pw = w.shape[2] - 3 * nh * dh
    aliased = 0 if kstack is None else 2
    hm = jax.ShapeDtypeStruct((b, nh, s, dh), BF16)
    hm_spec = pl.BlockSpec((bt, nh, ts, dh), lambda i, j: (i, 0, j, 0))
    stack = jax.ShapeDtypeStruct((depth, b, nh, s, dh), F32)
    stack_spec = pl.BlockSpec((None, bt, nh, ts, dh), lambda i, j: (l, i, 0, j, 0))
    any_spec = pl.BlockSpec(memory_space=pl.ANY)
    return pl.pallas_call(
        functools.partial(_in_proj_kernel, bt=bt, ts=ts, nh=nh, dh=dh, qscale=LOG2E * dh ** -0.5,
                          aliased=aliased),
        grid=(b // bt, s // ts),
        in_specs=[pl.BlockSpec((bt, ts, d), lambda i, j: (i, j, 0)),
                  _layer_spec(g, l), _layer_spec(w, l)] + [any_spec] * aliased,
        out_specs=[hm_spec, hm_spec, hm_spec,
                   pl.BlockSpec((bt, ts, pw), lambda i, j: (i, j, 0)), stack_spec, stack_spec],
        out_shape=[hm, hm, hm, jax.ShapeDtypeStruct((b, s, pw), F32), stack, stack],
        input_output_aliases={3: 4, 4: 5} if aliased else {},
        compiler_params=_params(("parallel", "parallel")),
        name="in_proj",
    )(x, g, w, *((kstack, vstack) if aliased else ()))


def _strict_lower(n):
    j = lax.broadcasted_iota(jnp.int32, (n, n), 0)
    s = lax.broadcasted_iota(jnp.int32, (n, n), 1)
    return (j > s).astype(BF16)


NT_DIMS = (((1,), (1,)), ((), ()))


def _softplus2(z):
    return jnp.where(z > SOFTPLUS_LINEAR, z, jnp.log(1.0 + jnp.exp2(z)) * LOG2E)


def _sb_tile(q, kb, vb, tmat, c, acc, mask, kv_transposed=False):
    nt = NT_DIMS
    if kv_transposed:
        z = jnp.dot(q, kb, preferred_element_type=F32)
    else:
        z = lax.dot_general(q, kb, nt, preferred_element_type=F32)
    if mask is not None:
        z = jnp.where(mask, z, MASKED_LOGIT)
    sp = _softplus2(z)
    lb = z - sp
    spb = sp.astype(BF16)
    pex = jnp.dot(spb, tmat, preferred_element_type=F32)
    a = jnp.exp2(lb - pex - c)
    if kv_transposed:
        acc = acc + lax.dot_general(a.astype(BF16), vb, nt, preferred_element_type=F32)
    else:
        acc = acc + jnp.dot(a.astype(BF16), vb, preferred_element_type=F32)
    c = c + pex[:, :1] + spb[:, :1].astype(F32)
    return c, acc


def _causal_mask(n):
    row = lax.broadcasted_iota(jnp.int32, (n, n), 0)
    col = lax.broadcasted_iota(jnp.int32, (n, n), 1)
    return col < row


def _cmin(state):
    return functools.reduce(jnp.minimum, [jnp.min(c) for c, _ in state])


def _attn_kernel(q_ref, k_ref, v_ref, t_ref, o_ref, c_ref, *, tq, hp, nsub):
    step_i = pl.program_id(2)
    dh = q_ref.shape[-1]
    chains = [(h, sub) for sub in range(nsub) for h in range(hp)]

    def tile(h, sub, j, c, acc, m=None):
        start = pl.multiple_of(j * tq, tq)
        return _sb_tile(q_ref[0, h, sub * tq:(sub + 1) * tq, :], k_ref[0, h, pl.ds(start, tq), :],
                        v_ref[0, h, pl.ds(start, tq), :], t_ref[...], c, acc, m)

    def diag(h, sub):
        return tile(h, sub, step_i * nsub + sub, jnp.zeros((tq, 1), F32), jnp.zeros((tq, dh), F32),
                    _causal_mask(tq))

    def pair(h, sub):
        return tile(h, sub, step_i * nsub + sub - 1, *diag(h, sub))

    def logits(h, sub):
        start = pl.multiple_of((step_i * nsub + sub - 1) * tq, tq)
        z = lax.dot_general(q_ref[0, h, sub * tq:(sub + 1) * tq, :],
                            k_ref[0, h, pl.ds(start, 2 * tq), :], NT_DIMS,
                            preferred_element_type=F32)
        z = jnp.concatenate([z[:, :tq], jnp.where(_causal_mask(tq), z[:, tq:], MASKED_LOGIT)], axis=1)
        sp = _softplus2(z)
        return h, start, z - sp, sp[:, :tq].astype(BF16), sp[:, tq:].astype(BF16)

    def suffix_sums(h, start, lb, spb_p, spb_d):
        pex_p = jnp.dot(spb_p, t_ref[...], preferred_element_type=F32)
        pex_d = jnp.dot(spb_d, t_ref[...], preferred_element_type=F32)
        c_d = pex_d[:, :1] + spb_d[:, :1].astype(F32)
        c_p = pex_p[:, :1] + spb_p[:, :1].astype(F32)
        return h, start, lb, pex_p, pex_d, c_d, c_p

    def weights(h, start, lb, pex_p, pex_d, c_d, c_p):
        a_p = jnp.exp2(lb[:, :tq] - pex_p - c_d)
        a_d = jnp.exp2(lb[:, tq:] - pex_d)
        return h, start, jnp.concatenate([a_p, a_d], axis=1).astype(BF16), c_d + c_p

    def output(h, start, a, c):
        return c, jnp.dot(a, v_ref[0, h, pl.ds(start, 2 * tq), :], preferred_element_type=F32)

    def pairs_by_stage():
        stages = (logits, suffix_sums, weights, output)
        vals = [(h, sub) for h, sub in chains]
        for t in range(len(chains) + len(stages) - 1):
            for s in reversed(range(len(stages))):
                n = t - s
                if 0 <= n < len(chains):
                    vals[n] = stages[s](*vals[n])
        return tuple(vals)

    def out_slice(h, sub):
        return (0, slice(sub * tq, (sub + 1) * tq), slice(h * dh, (h + 1) * dh))

    def park(state):
        for n, (h, sub) in enumerate(chains):
            c_ref[n] = state[n][0]
            o_ref[out_slice(h, sub)] = state[n][1]
        return tuple(_cmin(state[sub * hp:(sub + 1) * hp]) for sub in range(nsub))

    cmins = lax.cond(step_i == 0,
                     lambda: park(tuple(diag(h, sub) if sub == 0 else pair(h, sub)
                                        for h, sub in chains)),
                     lambda: park(pairs_by_stage()))

    def cond(st):
        return jnp.logical_and(st[0] >= 0, st[1] < EXP_UNDERFLOW)

    for sub in range(nsub):
        def body(st, sub=sub):
            cmins = []
            for h in range(hp):
                n = sub * hp + h
                c, acc = tile(h, sub, st[0], c_ref[n], o_ref[out_slice(h, sub)])
                c_ref[n] = c
                o_ref[out_slice(h, sub)] = acc
                cmins.append(jnp.min(c))
            return st[0] - 1, functools.reduce(jnp.minimum, cmins)

        first = step_i * nsub + sub - 2
        lax.while_loop(cond, body, (first, cmins[sub]))


def _attn(q, k, v, tmat, *, tq, hp, nsub):
    b, nh, s, dh = q.shape
    rows = nsub * tq
    return pl.pallas_call(
        functools.partial(_attn_kernel, tq=tq, hp=hp, nsub=nsub),
        grid=(b, nh // hp, s // rows),
        in_specs=[pl.BlockSpec((1, hp, rows, dh), lambda bi, hi, i: (bi, hi, i, 0)),
                  pl.BlockSpec((1, hp, s, dh), lambda bi, hi, i: (bi, hi, 0, 0)),
                  pl.BlockSpec((1, hp, s, dh), lambda bi, hi, i: (bi, hi, 0, 0)),
                  _const_spec(tmat.shape)],
        out_specs=pl.BlockSpec((1, rows, hp * dh), lambda bi, hi, i: (bi, i, hi)),
        out_shape=jax.ShapeDtypeStruct((b, s, nh * dh), F32),
        scratch_shapes=[pltpu.VMEM((nsub * hp, tq, 1), F32)],
        compiler_params=_params(("parallel", "parallel", "arbitrary")),
        name="attn",
    )(q, k, v, tmat)


def _attn_dec_kernel(q_ref, kn_ref, vn_ref, kp_ref, vp_ref, tn_ref, tp_ref, o_ref, *, tkp):
    nh, ts, dh = q_ref.shape[-3:]
    past = kp_ref.shape[-1]
    mask = _causal_mask(ts)

    def past_block(j, state):
        cols = slice(j * tkp, (j + 1) * tkp)
        return tuple(_sb_tile(q_ref[0, h], kp_ref[0, h, :, cols].astype(BF16),
                              vp_ref[0, h, :, cols].astype(BF16), tp_ref[...], *state[h], None,
                              kv_transposed=True)
                     for h in range(nh))

    state = tuple(_sb_tile(q_ref[0, h], kn_ref[0, h], vn_ref[0, h], tn_ref[...],
                           jnp.zeros((ts, 1), F32), jnp.zeros((ts, dh), F32), mask)
                  for h in range(nh))
    nblk = past // tkp
    state = past_block(nblk - 1, state)
    for j in reversed(range(nblk - 1)):
        state = lax.cond(_cmin(state) < EXP_UNDERFLOW,
                         functools.partial(past_block, j), lambda st: st, state)
    for h in range(nh):
        o_ref[0, :, h * dh:(h + 1) * dh] = state[h][1]


def _attn_dec(q, kn, vn, kp_t, vp_t, tn, tp, l):
    b, nh, ts, dh = q.shape
    past = kp_t.shape[4]
    new_spec = pl.BlockSpec((1, nh, ts, dh), lambda bi: (bi, 0, 0, 0))
    past_spec = pl.BlockSpec((None, 1, nh, dh, past), lambda bi: (l, bi, 0, 0, 0))
    return pl.pallas_call(
        functools.partial(_attn_dec_kernel, tkp=tp.shape[0]),
        grid=(b,),
        in_specs=[new_spec, new_spec, new_spec, past_spec, past_spec,
                  _const_spec(tn.shape), _const_spec(tp.shape)],
        out_specs=pl.BlockSpec((1, ts, nh * dh), lambda bi: (bi, 0, 0)),
        out_shape=jax.ShapeDtypeStruct((b, ts, nh * dh), F32),
        compiler_params=_params(("parallel",)),
        name="attn_dec",
    )(q, kn, vn, kp_t, vp_t, tn, tp)


def _mix_ffn_kernel(x_ref, att_ref, up_ref, halo_ref, p_ref, gsb_ref, wpool_ref, pscale_ref,
                    wout_ref, gffn_ref, wg_ref, wu_ref, wd_ref, gple_ref, wple_ref, wpg_ref,
                    gfin_ref, o_ref, ext_ref, *, bt, ts, pos_offset, zero_first_halo, final_norm):
    j = pl.program_id(1)
    d = x_ref.shape[-1]
    pw = up_ref.shape[-1]
    gd = pw // len(POOL_WINDOWS)
    pos = pos_offset + j * ts + lax.broadcasted_iota(jnp.int32, (ts, 1), 0)
    mixed = []
    for b in range(bt):
        att = _rms(att_ref[b], gsb_ref[...]).astype(BF16)

        halo = halo_ref[b]
        if zero_first_halo:
            halo = jnp.where(j == 0, 0.0, halo)
        up = up_ref[b]
        ext_ref[0:POOL_HALO, :] = halo
        ext_ref[POOL_HALO:, :] = up
        ys = []
        for g, w in enumerate(POOL_WINDOWS):
            cols = slice(g * gd, (g + 1) * gd)
            win = up[:, cols]
            for sh in range(1, w):
                win = win + ext_ref[POOL_HALO - sh:POOL_HALO - sh + ts, cols]
            cnt = jnp.minimum(w, pos + 1).astype(F32)
            dg = win / cnt - up[:, cols]
            ys.append(jnp.dot(dg.astype(BF16), wpool_ref[g], preferred_element_type=F32))
        pool = _rms(jnp.concatenate(ys, axis=-1), pscale_ref[...]).astype(BF16)

        sbw = att.shape[-1]
        mixed.append(x_ref[b] + jnp.dot(att, wout_ref[:sbw, :], preferred_element_type=F32)
                     + jnp.dot(pool, wout_ref[sbw:, :], preferred_element_type=F32))
    x = mixed[0] if bt == 1 else jnp.concatenate(mixed, axis=0)

    h = _rms(x, gffn_ref[...]).astype(BF16)
    gate = jnp.dot(h, wg_ref[...], preferred_element_type=F32)
    upv = jnp.dot(h, wu_ref[...], preferred_element_type=F32)
    hid = (gate * jax.nn.sigmoid(gate) * upv).astype(BF16)
    x = x + jnp.dot(hid, wd_ref[...], preferred_element_type=F32)
    hp = _rms(x, gple_ref[...]).astype(BF16)
    pg = jax.nn.sigmoid(jnp.dot(hp, wpg_ref[...], preferred_element_type=F32))
    p = p_ref[...].reshape(bt * ts, p_ref.shape[-1]).astype(BF16)
    x = x + jnp.dot(p, wple_ref[...], preferred_element_type=F32) * pg
    if final_norm:
        x = _rms(x, gfin_ref[...])
    o_ref[...] = x.reshape(bt, ts, d)


def _mix_ffn(x, att, up, halo_src, p, gsb, wpool, pscale, wout, gffn, wg, wu, wd, gple, wple, wpg,
             gfin, l, *, bt, ts, pos_offset, zero_first_halo, final_norm):
    b, s, d = x.shape
    sbw = att.shape[-1]
    pw = up.shape[-1]
    pd = p.shape[-1]
    if zero_first_halo:
        halo_spec = pl.BlockSpec((bt, POOL_HALO, pw),
                                 lambda i, j: (i, jnp.maximum(j * (ts // POOL_HALO) - 1, 0), 0))
    else:
        halo_spec = pl.BlockSpec((None, bt, POOL_HALO, pw), lambda i, j: (l, i, 0, 0))
    wspec = functools.partial(_layer_spec, l=l, buffers=1)
    tile = lambda width: pl.BlockSpec((bt, ts, width), lambda i, j: (i, j, 0))
    return pl.pallas_call(
        functools.partial(_mix_ffn_kernel, bt=bt, ts=ts, pos_offset=pos_offset,
                          zero_first_halo=zero_first_halo, final_norm=final_norm),
        grid=(b // bt, s // ts),
        in_specs=[tile(d), tile(sbw), tile(pw), halo_spec,
                  pl.BlockSpec((None, bt, ts, pd), lambda i, j: (l, i, j, 0)),
                  _layer_spec(gsb, l), wspec(wpool), _layer_spec(pscale, l), wspec(wout),
                  _layer_spec(gffn, l), wspec(wg), wspec(wu), wspec(wd),
                  _layer_spec(gple, l), wspec(wple), wspec(wpg), _const_spec(gfin.shape)],
        out_specs=tile(d),
        out_shape=jax.ShapeDtypeStruct((b, s, d), F32),
        scratch_shapes=[pltpu.VMEM((POOL_HALO + ts, pw), F32)],
        compiler_params=_params(("parallel", "arbitrary")),
        name="mix_ffn",
    )(x, att, up, halo_src, p, gsb, wpool, pscale, wout, gffn, wg, wu, wd, gple, wple, wpg, gfin)


def _forward(x_prompt, x_sample, cache_k, cache_v, state_pool, p_prompt, p_sample,
             g_mix, w_in, g_sb_out, w_pool, pool_scale, w_out,
             g_ffn, w_ffn_gate, w_ffn_up, w_ffn_down, g_ple, w_ple, w_ple_gate, g_final,
             *, ts_prompt, tq, hp, nsub, tkp):
    depth = w_in.shape[0]
    bp, sp_len, d = x_prompt.shape
    bs, ss_len, _ = x_sample.shape
    nh, past, dh = cache_k.shape[2:]
    hist = state_pool.shape[2]

    rows = lambda a: a.reshape(depth, 1, -1)
    g_mix, g_sb_out, pool_scale, g_ffn, g_ple = map(rows, (g_mix, g_sb_out, pool_scale, g_ffn, g_ple))
    w_in, w_pool, w_out, wg, wu, wd, wple, wpg = (
        a.astype(BF16) for a in (w_in, w_pool, w_out, w_ffn_gate, w_ffn_up, w_ffn_down,
                                 w_ple, w_ple_gate))
    gfin = g_final.reshape(1, -1)
    t_prompt, t_new, t_past = _strict_lower(tq), _strict_lower(ss_len), _strict_lower(tkp)
    state_pad = jnp.pad(state_pool, ((0, 0), (0, 0), (POOL_HALO - hist, 0), (0, 0)))
    cache_kt, cache_vt = jnp.swapaxes(cache_k, 3, 4), jnp.swapaxes(cache_v, 3, 4)

    xp, xs = x_prompt, x_sample
    kp32 = vp32 = ks32 = vs32 = None
    pool_p, pool_s = [], []
    for l in range(depth):
        last = l == depth - 1
        qp, kp, vp, upp, kp32, vp32 = _in_proj(xp, g_mix, w_in, kp32, vp32, l,
                                               bt=1, ts=ts_prompt, nh=nh, dh=dh)
        attp = _attn(qp, kp, vp, t_prompt, tq=tq, hp=hp, nsub=nsub)
        xp = _mix_ffn(xp, attp, upp, upp, p_prompt, g_sb_out, w_pool, pool_scale, w_out,
                      g_ffn, wg, wu, wd, g_ple, wple, wpg, gfin, l, bt=1, ts=ts_prompt,
                      pos_offset=0, zero_first_halo=True, final_norm=last)
        qs, ks, vs, ups, ks32, vs32 = _in_proj(xs, g_mix, w_in, ks32, vs32, l,
                                               bt=bs, ts=ss_len, nh=nh, dh=dh)
        atts = _attn_dec(qs, ks, vs, cache_kt, cache_vt, t_new, t_past, l)
        xs = _mix_ffn(xs, atts, ups, state_pad, p_sample, g_sb_out, w_pool, pool_scale, w_out,
                      g_ffn, wg, wu, wd, g_ple, wple, wpg, gfin, l, bt=bs, ts=ss_len,
                      pos_offset=past, zero_first_halo=False, final_norm=last)

        pool_p.append(upp[:, -hist:])
        pool_s.append(jnp.concatenate([state_pool[l], ups], axis=1)[:, -hist:])

    return xp, xs, kp32, vp32, jnp.stack(pool_p), ks32, vs32, jnp.stack(pool_s)


def kernel(x_prompt, x_sample, cache_k, cache_v, state_pool, p_prompt, p_sample, g_mix, w_in, g_sb_out, w_pool, pool_scale, w_out, g_ffn, w_ffn_gate, w_ffn_up, w_ffn_down, g_ple, w_ple, w_ple_gate, g_final):
    return _forward(x_prompt, x_sample, cache_k, cache_v, state_pool, p_prompt, p_sample,
                    g_mix, w_in, g_sb_out, w_pool, pool_scale, w_out,
                    g_ffn, w_ffn_gate, w_ffn_up, w_ffn_down, g_ple, w_ple, w_ple_gate, g_final,
                    ts_prompt=512, tq=256, hp=4, nsub=2, tkp=256)
```

```python
import functools

import jax
import jax.numpy as jnp
from jax import lax
from jax.experimental import pallas as pl
from jax.experimental.pallas import tpu as pltpu

F32 = jnp.float32
BF16 = jnp.bfloat16

EPS = 1e-6
POOL_WINDOWS = (2, 4, 8, 16)
POOL_HALO = 16
VMEM_LIMIT_BYTES = 56 * 1024 * 1024
EXP_UNDERFLOW = 160.0
LOG2E = 1.4426950408889634
SOFTPLUS_LINEAR = 64.0
MASKED_LOGIT = -1e30


def _rms(x, g):
    ms = jnp.mean(x * x, axis=-1, keepdims=True)
    return (x * lax.rsqrt(ms + EPS)) * g


def _layer_spec(arr, l, buffers=None):
    shape = arr.shape[1:]
    index_map = lambda *_: (l,) + (0,) * len(shape)
    if buffers is None:
        return pl.BlockSpec((None,) + shape, index_map)
    return pl.BlockSpec((None,) + shape, index_map, pipeline_mode=pl.Buffered(buffers))


def _const_spec(shape):
    return pl.BlockSpec(shape, lambda *_: (0,) * len(shape))


def _params(sem):
    return pltpu.CompilerParams(dimension_semantics=sem, vmem_limit_bytes=VMEM_LIMIT_BYTES)


def _in_proj_kernel(*refs, bt, ts, nh, dh, qscale, aliased):
    x_ref, g_ref, w_ref = refs[:3]
    q_ref, kb_ref, vb_ref, up_ref, k32_ref, v32_ref = refs[3 + aliased:]
    d = x_ref.shape[-1]
    sbw = nh * dh
    x = x_ref[...].reshape(bt * ts, d)
    h = _rms(x, g_ref[...]).astype(BF16)

    def heads(part):
        u = jnp.dot(h, w_ref[:, part * sbw:(part + 1) * sbw], preferred_element_type=F32)
        for b in range(bt):
            for hh in range(nh):
                yield b, hh, u[b * ts:(b + 1) * ts, hh * dh:(hh + 1) * dh]

    for b, hh, u in heads(0):
        q_ref[b, hh] = (u * qscale).astype(BF16)
    for b, hh, u in heads(1):
        k32_ref[b, hh] = u
        kb_ref[b, hh] = u.astype(BF16)
    for b, hh, u in heads(2):
        v32_ref[b, hh] = u
        vb_ref[b, hh] = u.astype(BF16)
    u = jnp.dot(h, w_ref[:, 3 * sbw:], preferred_element_type=F32)
    for b in range(bt):
        up_ref[b] = u[b * ts:(b + 1) * ts, :]


def _in_proj(x, g, w, kstack, vstack, l, *, bt, ts, nh, dh):
    b, s, d = x.shape
    depth = w.shape[0]
    pw = w.shape[2] - 3 * nh * dh
    aliased = 0 if kstack is None else 2
    hm = jax.ShapeDtypeStruct((b, nh, s, dh), BF16)
    hm_spec = pl.BlockSpec((bt, nh, ts, dh), lambda i, j: (i, 0, j, 0))
    stack = jax.ShapeDtypeStruct((depth, b, nh, s, dh), F32)
    stack_spec = pl.BlockSpec((None, bt, nh, ts, dh), lambda i, j: (l, i, 0, j, 0))
    any_spec = pl.BlockSpec(memory_space=pl.ANY)
    return pl.pallas_call(
        functools.partial(_in_proj_kernel, bt=bt, ts=ts, nh=nh, dh=dh, qscale=LOG2E * dh ** -0.5,
                          aliased=aliased),
        grid=(b // bt, s // ts),
        in_specs=[pl.BlockSpec((bt, ts, d), lambda i, j: (i, j, 0)),
                  _layer_spec(g, l), _layer_spec(w, l)] + [any_spec] * aliased,
        out_specs=[hm_spec, hm_spec, hm_spec,
                   pl.BlockSpec((bt, ts, pw), lambda i, j: (i, j, 0)), stack_spec, stack_spec],
        out_shape=[hm, hm, hm, jax.ShapeDtypeStruct((b, s, pw), F32), stack, stack],
        input_output_aliases={3: 4, 4: 5} if aliased else {},
        compiler_params=_params(("parallel", "parallel")),
        name="in_proj",
    )(x, g, w, *((kstack, vstack) if aliased else ()))


def _strict_lower(n):
    j = lax.broadcasted_iota(jnp.int32, (n, n), 0)
    s = lax.broadcasted_iota(jnp.int32, (n, n), 1)
    return (j > s).astype(BF16)


NT_DIMS = (((1,), (1,)), ((), ()))


def _softplus2(z):
    return jnp.where(z > SOFTPLUS_LINEAR, z, jnp.log(1.0 + jnp.exp2(z)) * LOG2E)


def _sb_tile(q, kb, vb, tmat, c, acc, mask, kv_transposed=False):
    nt = NT_DIMS
    if kv_transposed:
        z = jnp.dot(q, kb, preferred_element_type=F32)
    else:
        z = lax.dot_general(q, kb, nt, preferred_element_type=F32)
    if mask is not None:
        z = jnp.where(mask, z, MASKED_LOGIT)
    sp = _softplus2(z)
    lb = z - sp
    spb = sp.astype(BF16)
    pex = jnp.dot(spb, tmat, preferred_element_type=F32)
    a = jnp.exp2(lb - pex - c)
    if kv_transposed:
        acc = acc + lax.dot_general(a.astype(BF16), vb, nt, preferred_element_type=F32)
    else:
        acc = acc + jnp.dot(a.astype(BF16), vb, preferred_element_type=F32)
    c = c + pex[:, :1] + spb[:, :1].astype(F32)
    return c, acc


def _causal_mask(n):
    row = lax.broadcasted_iota(jnp.int32, (n, n), 0)
    col = lax.broadcasted_iota(jnp.int32, (n, n), 1)
    return col < row


def _cmin(state):
    return functools.reduce(jnp.minimum, [jnp.min(c) for c, _ in state])


def _attn_kernel(q_ref, k_ref, v_ref, t_ref, o_ref, c_ref, *, tq, hp, nsub):
    step_i = pl.program_id(2)
    dh = q_ref.shape[-1]
    chains = [(h, sub) for sub in range(nsub) for h in range(hp)]

    def tile(h, sub, j, c, acc, m=None):
        start = pl.multiple_of(j * tq, tq)
        return _sb_tile(q_ref[0, h, sub * tq:(sub + 1) * tq, :], k_ref[0, h, pl.ds(start, tq), :],
                        v_ref[0, h, pl.ds(start, tq), :], t_ref[...], c, acc, m)

    def diag(h, sub):
        return tile(h, sub, step_i * nsub + sub, jnp.zeros((tq, 1), F32), jnp.zeros((tq, dh), F32),
                    _causal_mask(tq))

    def pair(h, sub):
        return tile(h, sub, step_i * nsub + sub - 1, *diag(h, sub))

    def logits(h, sub):
        start = pl.multiple_of((step_i * nsub + sub - 1) * tq, tq)
        z = lax.dot_general(q_ref[0, h, sub * tq:(sub + 1) * tq, :],
                            k_ref[0, h, pl.ds(start, 2 * tq), :], NT_DIMS,
                            preferred_element_type=F32)
        z = jnp.concatenate([z[:, :tq], jnp.where(_causal_mask(tq), z[:, tq:], MASKED_LOGIT)], axis=1)
        sp = _softplus2(z)
        return h, start, z - sp, sp[:, :tq].astype(BF16), sp[:, tq:].astype(BF16)

    def suffix_sums(h, start, lb, spb_p, spb_d):
        pex_p = jnp.dot(spb_p, t_ref[...], preferred_element_type=F32)
        pex_d = jnp.dot(spb_d, t_ref[...], preferred_element_type=F32)
        c_d = pex_d[:, :1] + spb_d[:, :1].astype(F32)
        c_p = pex_p[:, :1] + spb_p[:, :1].astype(F32)
        return h, start, lb, pex_p, pex_d, c_d, c_p

    def weights(h, start, lb, pex_p, pex_d, c_d, c_p):
        a_p = jnp.exp2(lb[:, :tq] - pex_p - c_d)
        a_d = jnp.exp2(lb[:, tq:] - pex_d)
        return h, start, jnp.concatenate([a_p, a_d], axis=1).astype(BF16), c_d + c_p

    def output(h, start, a, c):
        return c, jnp.dot(a, v_ref[0, h, pl.ds(start, 2 * tq), :], preferred_element_type=F32)

    def pairs_by_stage():
        stages = (logits, suffix_sums, weights, output)
        vals = [(h, sub) for h, sub in chains]
        for t in range(len(chains) + len(stages) - 1):
            for s in reversed(range(len(stages))):
                n = t - s
                if 0 <= n < len(chains):
                    vals[n] = stages[s](*vals[n])
        return tuple(vals)

    def out_slice(h, sub):
        return (0, slice(sub * tq, (sub + 1) * tq), slice(h * dh, (h + 1) * dh))

    def park(state):
        for n, (h, sub) in enumerate(chains):
            c_ref[n] = state[n][0]
            o_ref[out_slice(h, sub)] = state[n][1]
        return tuple(_cmin(state[sub * hp:(sub + 1) * hp]) for sub in range(nsub))

    cmins = lax.cond(step_i == 0,
                     lambda: park(tuple(diag(h, sub) if sub == 0 else pair(h, sub)
                                        for h, sub in chains)),
                     lambda: park(pairs_by_stage()))

    def cond(st):
        return jnp.logical_and(st[0] >= 0, st[1] < EXP_UNDERFLOW)

    for sub in range(nsub):
        def body(st, sub=sub):
            cmins = []
            for h in range(hp):
                n = sub * hp + h
                c, acc = tile(h, sub, st[0], c_ref[n], o_ref[out_slice(h, sub)])
                c_ref[n] = c
                o_ref[out_slice(h, sub)] = acc
                cmins.append(jnp.min(c))
            return st[0] - 1, functools.reduce(jnp.minimum, cmins)

        first = step_i * nsub + sub - 2
        lax.while_loop(cond, body, (first, cmins[sub]))


def _attn(q, k, v, tmat, *, tq, hp, nsub):
    b, nh, s, dh = q.shape
    rows = nsub * tq
    return pl.pallas_call(
        functools.partial(_attn_kernel, tq=tq, hp=hp, nsub=nsub),
        grid=(b, nh // hp, s // rows),
        in_specs=[pl.BlockSpec((1, hp, rows, dh), lambda bi, hi, i: (bi, hi, i, 0)),
                  pl.BlockSpec((1, hp, s, dh), lambda bi, hi, i: (bi, hi, 0, 0)),
                  pl.BlockSpec((1, hp, s, dh), lambda bi, hi, i: (bi, hi, 0, 0)),
                  _const_spec(tmat.shape)],
        out_specs=pl.BlockSpec((1, rows, hp * dh), lambda bi, hi, i: (bi, i, hi)),
        out_shape=jax.ShapeDtypeStruct((b, s, nh * dh), F32),
        scratch_shapes=[pltpu.VMEM((nsub * hp, tq, 1), F32)],
        compiler_params=_params(("parallel", "parallel", "arbitrary")),
        name="attn",
    )(q, k, v, tmat)


def _attn_dec_kernel(q_ref, kn_ref, vn_ref, kp_ref, vp_ref, tn_ref, tp_ref, o_ref, *, tkp):
    nh, ts, dh = q_ref.shape[-3:]
    past = kp_ref.shape[-1]
    mask = _causal_mask(ts)

    def past_block(j, state):
        cols = slice(j * tkp, (j + 1) * tkp)
        return tuple(_sb_tile(q_ref[0, h], kp_ref[0, h, :, cols].astype(BF16),
                              vp_ref[0, h, :, cols].astype(BF16), tp_ref[...], *state[h], None,
                              kv_transposed=True)
                     for h in range(nh))

    state = tuple(_sb_tile(q_ref[0, h], kn_ref[0, h], vn_ref[0, h], tn_ref[...],
                           jnp.zeros((ts, 1), F32), jnp.zeros((ts, dh), F32), mask)
                  for h in range(nh))
    nblk = past // tkp
    state = past_block(nblk - 1, state)
    for j in reversed(range(nblk - 1)):
        state = lax.cond(_cmin(state) < EXP_UNDERFLOW,
                         functools.partial(past_block, j), lambda st: st, state)
    for h in range(nh):
        o_ref[0, :, h * dh:(h + 1) * dh] = state[h][1]


def _attn_dec(q, kn, vn, kp_t, vp_t, tn, tp, l):
    b, nh, ts, dh = q.shape
    past = kp_t.shape[4]
    new_spec = pl.BlockSpec((1, nh, ts, dh), lambda bi: (bi, 0, 0, 0))
    past_spec = pl.BlockSpec((None, 1, nh, dh, past), lambda bi: (l, bi, 0, 0, 0))
    return pl.pallas_call(
        functools.partial(_attn_dec_kernel, tkp=tp.shape[0]),
        grid=(b,),
        in_specs=[new_spec, new_spec, new_spec, past_spec, past_spec,
                  _const_spec(tn.shape), _const_spec(tp.shape)],
        out_specs=pl.BlockSpec((1, ts, nh * dh), lambda bi: (bi, 0, 0)),
        out_shape=jax.ShapeDtypeStruct((b, ts, nh * dh), F32),
        compiler_params=_params(("parallel",)),
        name="attn_dec",
    )(q, kn, vn, kp_t, vp_t, tn, tp)


def _mix_ffn_kernel(x_ref, att_ref, up_ref, halo_ref, p_ref, gsb_ref, wpool_ref, pscale_ref,
                    wout_ref, gffn_ref, wg_ref, wu_ref, wd_ref, gple_ref, wple_ref, wpg_ref,
                    gfin_ref, o_ref, ext_ref, *, bt, ts, pos_offset, zero_first_halo, final_norm):
    j = pl.program_id(1)
    d = x_ref.shape[-1]
    pw = up_ref.shape[-1]
    gd = pw // len(POOL_WINDOWS)
    pos = pos_offset + j * ts + lax.broadcasted_iota(jnp.int32, (ts, 1), 0)
    mixed = []
    for b in range(bt):
        att = _rms(att_ref[b], gsb_ref[...]).astype(BF16)

        halo = halo_ref[b]
        if zero_first_halo:
            halo = jnp.where(j == 0, 0.0, halo)
        up = up_ref[b]
        ext_ref[0:POOL_HALO, :] = halo
        ext_ref[POOL_HALO:, :] = up
        ys = []
        for g, w in enumerate(POOL_WINDOWS):
            cols = slice(g * gd, (g + 1) * gd)
            win = up[:, cols]
            for sh in range(1, w):
                win = win + ext_ref[POOL_HALO - sh:POOL_HALO - sh + ts, cols]
            cnt = jnp.minimum(w, pos + 1).astype(F32)
            dg = win / cnt - up[:, cols]
            ys.append(jnp.dot(dg.astype(BF16), wpool_ref[g], preferred_element_type=F32))
        pool = _rms(jnp.concatenate(ys, axis=-1), pscale_ref[...]).astype(BF16)

        sbw = att.shape[-1]
        mixed.append(x_ref[b] + jnp.dot(att, wout_ref[:sbw, :], preferred_element_type=F32)
                     + jnp.dot(pool, wout_ref[sbw:, :], preferred_element_type=F32))
    x = mixed[0] if bt == 1 else jnp.concatenate(mixed, axis=0)

    h = _rms(x, gffn_ref[...]).astype(BF16)
    gate = jnp.dot(h, wg_ref[...], preferred_element_type=F32)
    upv = jnp.dot(h, wu_ref[...], preferred_element_type=F32)
    hid = (gate * jax.nn.sigmoid(gate) * upv).astype(BF16)
    x = x + jnp.dot(hid, wd_ref[...], preferred_element_type=F32)
    hp = _rms(x, gple_ref[...]).astype(BF16)
    pg = jax.nn.sigmoid(jnp.dot(hp, wpg_ref[...], preferred_element_type=F32))
    p = p_ref[...].reshape(bt * ts, p_ref.shape[-1]).astype(BF16)
    x = x + jnp.dot(p, wple_ref[...], preferred_element_type=F32) * pg
    if final_norm:
        x = _rms(x, gfin_ref[...])
    o_ref[...] = x.reshape(bt, ts, d)


def _mix_ffn(x, att, up, halo_src, p, gsb, wpool, pscale, wout, gffn, wg, wu, wd, gple, wple, wpg,
             gfin, l, *, bt, ts, pos_offset, zero_first_halo, final_norm):
    b, s, d = x.shape
    sbw = att.shape[-1]
    pw = up.shape[-1]
    pd = p.shape[-1]
    if zero_first_halo:
        halo_spec = pl.BlockSpec((bt, POOL_HALO, pw),
                                 lambda i, j: (i, jnp.maximum(j * (ts // POOL_HALO) - 1, 0), 0))
    else:
        halo_spec = pl.BlockSpec((None, bt, POOL_HALO, pw), lambda i, j: (l, i, 0, 0))
    wspec = functools.partial(_layer_spec, l=l, buffers=1)
    tile = lambda width: pl.BlockSpec((bt, ts, width), lambda i, j: (i, j, 0))
    return pl.pallas_call(
        functools.partial(_mix_ffn_kernel, bt=bt, ts=ts, pos_offset=pos_offset,
                          zero_first_halo=zero_first_halo, final_norm=final_norm),
        grid=(b // bt, s // ts),
        in_specs=[tile(d), tile(sbw), tile(pw), halo_spec,
                  pl.BlockSpec((None, bt, ts, pd), lambda i, j: (l, i, j, 0)),
                  _layer_spec(gsb, l), wspec(wpool), _layer_spec(pscale, l), wspec(wout),
                  _layer_spec(gffn, l), wspec(wg), wspec(wu), wspec(wd),
                  _layer_spec(gple, l), wspec(wple), wspec(wpg), _const_spec(gfin.shape)],
        out_specs=tile(d),
        out_shape=jax.ShapeDtypeStruct((b, s, d), F32),
        scratch_shapes=[pltpu.VMEM((POOL_HALO + ts, pw), F32)],
        compiler_params=_params(("parallel", "arbitrary")),
        name="mix_ffn",
    )(x, att, up, halo_src, p, gsb, wpool, pscale, wout, gffn, wg, wu, wd, gple, wple, wpg, gfin)


def _forward(x_prompt, x_sample, cache_k, cache_v, state_pool, p_prompt, p_sample,
             g_mix, w_in, g_sb_out, w_pool, pool_scale, w_out,
             g_ffn, w_ffn_gate, w_ffn_up, w_ffn_down, g_ple, w_ple, w_ple_gate, g_final,
             *, ts_prompt, tq, hp, nsub, tkp):
    depth = w_in.shape[0]
    bp, sp_len, d = x_prompt.shape
    bs, ss_len, _ = x_sample.shape
    nh, past, dh = cache_k.shape[2:]
    hist = state_pool.shape[2]

    rows = lambda a: a.reshape(depth, 1, -1)
    g_mix, g_sb_out, pool_scale, g_ffn, g_ple = map(rows, (g_mix, g_sb_out, pool_scale, g_ffn, g_ple))
    w_in, w_pool, w_out, wg, wu, wd, wple, wpg = (
        a.astype(BF16) for a in (w_in, w_pool, w_out, w_ffn_gate, w_ffn_up, w_ffn_down,
                                 w_ple, w_ple_gate))
    gfin = g_final.reshape(1, -1)
    t_prompt, t_new, t_past = _strict_lower(tq), _strict_lower(ss_len), _strict_lower(tkp)
    state_pad = jnp.pad(state_pool, ((0, 0), (0, 0), (POOL_HALO - hist, 0), (0, 0)))
    cache_kt, cache_vt = jnp.swapaxes(cache_k, 3, 4), jnp.swapaxes(cache_v, 3, 4)

    xp, xs = x_prompt, x_sample
    kp32 = vp32 = ks32 = vs32 = None
    pool_p, pool_s = [], []
    for l in range(depth):
        last = l == depth - 1
        qp, kp, vp, upp, kp32, vp32 = _in_proj(xp, g_mix, w_in, kp32, vp32, l,
                                               bt=1, ts=ts_prompt, nh=nh, dh=dh)
        attp = _attn(qp, kp, vp, t_prompt, tq=tq, hp=hp, nsub=nsub)
        xp = _mix_ffn(xp, attp, upp, upp, p_prompt, g_sb_out, w_pool, pool_scale, w_out,
                      g_ffn, wg, wu, wd, g_ple, wple, wpg, gfin, l, bt=1, ts=ts_prompt,
                      pos_offset=0, zero_first_halo=True, final_norm=last)
        qs, ks, vs, ups, ks32, vs32 = _in_proj(xs, g_mix, w_in, ks32, vs32, l,
                                               bt=bs, ts=ss_len, nh=nh, dh=dh)
        atts = _attn_dec(qs, ks, vs, cache_kt, cache_vt, t_new, t_past, l)
        xs = _mix_ffn(xs, atts, ups, state_pad, p_sample, g_sb_out, w_pool, pool_scale, w_out,
                      g_ffn, wg, wu, wd, g_ple, wple, wpg, gfin, l, bt=bs, ts=ss_len,
                      pos_offset=past, zero_first_halo=False, final_norm=last)

        pool_p.append(upp[:, -hist:])
        pool_s.append(jnp.concatenate([state_pool[l], ups], axis=1)[:, -hist:])

    return xp, xs, kp32, vp32, jnp.stack(pool_p), ks32, vs32, jnp.stack(pool_s)


def kernel(x_prompt, x_sample, cache_k, cache_v, state_pool, p_prompt, p_sample, g_mix, w_in, g_sb_out, w_pool, pool_scale, w_out, g_ffn, w_ffn_gate, w_ffn_up, w_ffn_down, g_ple, w_ple, w_ple_gate, g_final):
    return _forward(x_prompt, x_sample, cache_k, cache_v, state_pool, p_prompt, p_sample,
                    g_mix, w_in, g_sb_out, w_pool, pool_scale, w_out,
                    g_ffn, w_ffn_gate, w_ffn_up, w_ffn_down, g_ple, w_ple, w_ple_gate, g_final,
                    ts_prompt=512, tq=256, hp=4, nsub=4, tkp=256)
```

```python
import functools

import jax
import jax.numpy as jnp
from jax import lax
from jax.experimental import pallas as pl
from jax.experimental.pallas import tpu as pltpu

F32 = jnp.float32
BF16 = jnp.bfloat16

EPS = 1e-6
POOL_WINDOWS = (2, 4, 8, 16)
POOL_HALO = 16
VMEM_LIMIT_BYTES = 56 * 1024 * 1024
EXP_UNDERFLOW = 160.0
LOG2E = 1.4426950408889634
SOFTPLUS_LINEAR = 64.0
MASKED_LOGIT = -1e30


def _rms(x, g):
    ms = jnp.mean(x * x, axis=-1, keepdims=True)
    return (x * lax.rsqrt(ms + EPS)) * g


def _layer_spec(arr, l, buffers=None):
    shape = arr.shape[1:]
    index_map = lambda *_: (l,) + (0,) * len(shape)
    if buffers is None:
        return pl.BlockSpec((None,) + shape, index_map)
    return pl.BlockSpec((None,) + shape, index_map, pipeline_mode=pl.Buffered(buffers))


def _const_spec(shape):
    return pl.BlockSpec(shape, lambda *_: (0,) * len(shape))


def _params(sem):
    return pltpu.CompilerParams(dimension_semantics=sem, vmem_limit_bytes=VMEM_LIMIT_BYTES)


def _in_proj_kernel(*refs, bt, ts, nh, dh, qscale, aliased):
    x_ref, g_ref, w_ref = refs[:3]
    q_ref, kb_ref, vb_ref, up_ref, k32_ref, v32_ref = refs[3 + aliased:]
    d = x_ref.shape[-1]
    sbw = nh * dh
    x = x_ref[...].reshape(bt * ts, d)
    h = _rms(x, g_ref[...]).astype(BF16)

    def heads(part):
        u = jnp.dot(h, w_ref[:, part * sbw:(part + 1) * sbw], preferred_element_type=F32)
        for b in range(bt):
            for hh in range(nh):
                yield b, hh, u[b * ts:(b + 1) * ts, hh * dh:(hh + 1) * dh]

    for b, hh, u in heads(0):
        q_ref[b, hh] = (u * qscale).astype(BF16)
    for b, hh, u in heads(1):
        k32_ref[b, hh] = u
        kb_ref[b, hh] = u.astype(BF16)
    for b, hh, u in heads(2):
        v32_ref[b, hh] = u
        vb_ref[b, hh] = u.astype(BF16)
    u = jnp.dot(h, w_ref[:, 3 * sbw:], preferred_element_type=F32)
    for b in range(bt):
        up_ref[b] = u[b * ts:(b + 1) * ts, :]


def _in_proj(x, g, w, kstack, vstack, l, *, bt, ts, nh, dh):
    b, s, d = x.shape
    depth = w.shape[0]
    pw = w.shape[2] - 3 * nh * dh
    aliased = 0 if kstack is None else 2
    hm = jax.ShapeDtypeStruct((b, nh, s, dh), BF16)
    hm_spec = pl.BlockSpec((bt, nh, ts, dh), lambda i, j: (i, 0, j, 0))
    stack = jax.ShapeDtypeStruct((depth, b, nh, s, dh), F32)
    stack_spec = pl.BlockSpec((None, bt, nh, ts, dh), lambda i, j: (l, i, 0, j, 0))
    any_spec = pl.BlockSpec(memory_space=pl.ANY)
    return pl.pallas_call(
        functools.partial(_in_proj_kernel, bt=bt, ts=ts, nh=nh, dh=dh, qscale=LOG2E * dh ** -0.5,
                          aliased=aliased),
        grid=(b // bt, s // ts),
        in_specs=[pl.BlockSpec((bt, ts, d), lambda i, j: (i, j, 0)),
                  _layer_spec(g, l), _layer_spec(w, l)] + [any_spec] * aliased,
        out_specs=[hm_spec, hm_spec, hm_spec,
                   pl.BlockSpec((bt, ts, pw), lambda i, j: (i, j, 0)), stack_spec, stack_spec],
        out_shape=[hm, hm, hm, jax.ShapeDtypeStruct((b, s, pw), F32), stack, stack],
        input_output_aliases={3: 4, 4: 5} if aliased else {},
        compiler_params=_params(("parallel", "parallel")),
        name="in_proj",
    )(x, g, w, *((kstack, vstack) if aliased else ()))


def _strict_lower(n):
    j = lax.broadcasted_iota(jnp.int32, (n, n), 0)
    s = lax.broadcasted_iota(jnp.int32, (n, n), 1)
    return (j > s).astype(BF16)


NT_DIMS = (((1,), (1,)), ((), ()))


def _softplus2(z):
    return jnp.where(z > SOFTPLUS_LINEAR, z, jnp.log(1.0 + jnp.exp2(z)) * LOG2E)


def _sb_tile(q, kb, vb, tmat, c, acc, mask, kv_transposed=False):
    nt = NT_DIMS
    if kv_transposed:
        z = jnp.dot(q, kb, preferred_element_type=F32)
    else:
        z = lax.dot_general(q, kb, nt, preferred_element_type=F32)
    if mask is not None:
        z = jnp.where(mask, z, MASKED_LOGIT)
    sp = _softplus2(z)
    lb = z - sp
    spb = sp.astype(BF16)
    pex = jnp.dot(spb, tmat, preferred_element_type=F32)
    a = jnp.exp2(lb - pex - c)
    if kv_transposed:
        acc = acc + lax.dot_general(a.astype(BF16), vb, nt, preferred_element_type=F32)
    else:
        acc = acc + jnp.dot(a.astype(BF16), vb, preferred_element_type=F32)
    c = c + pex[:, :1] + spb[:, :1].astype(F32)
    return c, acc


def _causal_mask(n):
    row = lax.broadcasted_iota(jnp.int32, (n, n), 0)
    col = lax.broadcasted_iota(jnp.int32, (n, n), 1)
    return col < row


def _cmin(state):
    return functools.reduce(jnp.minimum, [jnp.min(c) for c, _ in state])


def _attn_kernel(q_ref, k_ref, v_ref, t_ref, o_ref, c_ref, *, tq, hp, nsub):
    step_i = pl.program_id(2)
    dh = q_ref.shape[-1]
    chains = [(h, sub) for sub in range(nsub) for h in range(hp)]

    def tile(h, sub, j, c, acc, m=None):
        start = pl.multiple_of(j * tq, tq)
        return _sb_tile(q_ref[0, h, sub * tq:(sub + 1) * tq, :], k_ref[0, h, pl.ds(start, tq), :],
                        v_ref[0, h, pl.ds(start, tq), :], t_ref[...], c, acc, m)

    def diag(h, sub):
        return tile(h, sub, step_i * nsub + sub, jnp.zeros((tq, 1), F32), jnp.zeros((tq, dh), F32),
                    _causal_mask(tq))

    def logits(h, sub):
        start = pl.multiple_of((step_i * nsub + sub - 1) * tq, tq)
        z = lax.dot_general(q_ref[0, h, sub * tq:(sub + 1) * tq, :],
                            k_ref[0, h, pl.ds(start, 2 * tq), :], NT_DIMS,
                            preferred_element_type=F32)
        z = jnp.concatenate([z[:, :tq], jnp.where(_causal_mask(tq), z[:, tq:], MASKED_LOGIT)], axis=1)
        sp = _softplus2(z)
        return h, start, z - sp, sp[:, :tq].astype(BF16), sp[:, tq:].astype(BF16)

    def suffix_sums(h, start, lb, spb_p, spb_d):
        pex_p = jnp.dot(spb_p, t_ref[...], preferred_element_type=F32)
        pex_d = jnp.dot(spb_d, t_ref[...], preferred_element_type=F32)
        c_d = pex_d[:, :1] + spb_d[:, :1].astype(F32)
        c_p = pex_p[:, :1] + spb_p[:, :1].astype(F32)
        return h, start, lb, pex_p, pex_d, c_d, c_p

    def weights(h, start, lb, pex_p, pex_d, c_d, c_p):
        a_p = jnp.exp2(lb[:, :tq] - pex_p - c_d)
        a_d = jnp.exp2(lb[:, tq:] - pex_d)
        return h, start, jnp.concatenate([a_p, a_d], axis=1).astype(BF16), c_d + c_p

    def output(h, start, a, c):
        return c, jnp.dot(a, v_ref[0, h, pl.ds(start, 2 * tq), :], preferred_element_type=F32)

    def pairs_by_stage(which):
        stages = (logits, suffix_sums, weights, output)
        vals = list(which)
        for t in range(len(vals) + len(stages) - 1):
            for s in reversed(range(len(stages))):
                n = t - s
                if 0 <= n < len(vals):
                    vals[n] = stages[s](*vals[n])
        return tuple(vals)

    def out_slice(h, sub):
        return (0, slice(sub * tq, (sub + 1) * tq), slice(h * dh, (h + 1) * dh))

    def park(state):
        for n, (h, sub) in enumerate(chains):
            c_ref[n] = state[n][0]
            o_ref[out_slice(h, sub)] = state[n][1]
        return tuple(_cmin(state[sub * hp:(sub + 1) * hp]) for sub in range(nsub))

    cmins = lax.cond(step_i == 0,
                     lambda: park(tuple(diag(h, 0) for h in range(hp))
                                  + pairs_by_stage(chains[hp:])),
                     lambda: park(pairs_by_stage(chains)))

    def cond(st):
        return jnp.logical_and(st[0] >= 0, st[1] < EXP_UNDERFLOW)

    for sub in range(nsub):
        def body(st, sub=sub):
            cmins = []
            for h in range(hp):
                n = sub * hp + h
                c, acc = tile(h, sub, st[0], c_ref[n], o_ref[out_slice(h, sub)])
                c_ref[n] = c
                o_ref[out_slice(h, sub)] = acc
                cmins.append(jnp.min(c))
            return st[0] - 1, functools.reduce(jnp.minimum, cmins)

        first = step_i * nsub + sub - 2
        lax.while_loop(cond, body, (first, cmins[sub]))


def _attn(q, k, v, tmat, *, tq, hp, nsub):
    b, nh, s, dh = q.shape
    rows = nsub * tq
    return pl.pallas_call(
        functools.partial(_attn_kernel, tq=tq, hp=hp, nsub=nsub),
        grid=(b, nh // hp, s // rows),
        in_specs=[pl.BlockSpec((1, hp, rows, dh), lambda bi, hi, i: (bi, hi, i, 0)),
                  pl.BlockSpec((1, hp, s, dh), lambda bi, hi, i: (bi, hi, 0, 0)),
                  pl.BlockSpec((1, hp, s, dh), lambda bi, hi, i: (bi, hi, 0, 0)),
                  _const_spec(tmat.shape)],
        out_specs=pl.BlockSpec((1, rows, hp * dh), lambda bi, hi, i: (bi, i, hi)),
        out_shape=jax.ShapeDtypeStruct((b, s, nh * dh), F32),
        scratch_shapes=[pltpu.VMEM((nsub * hp, tq, 1), F32)],
        compiler_params=_params(("parallel", "parallel", "arbitrary")),
        name="attn",
    )(q, k, v, tmat)


def _attn_dec_kernel(q_ref, kn_ref, vn_ref, kp_ref, vp_ref, tn_ref, tp_ref, o_ref, *, tkp):
    nh, ts, dh = q_ref.shape[-3:]
    past = kp_ref.shape[-1]
    mask = _causal_mask(ts)

    def past_block(j, state):
        cols = slice(j * tkp, (j + 1) * tkp)
        return tuple(_sb_tile(q_ref[0, h], kp_ref[0, h, :, cols].astype(BF16),
                              vp_ref[0, h, :, cols].astype(BF16), tp_ref[...], *state[h], None,
                              kv_transposed=True)
                     for h in range(nh))

    state = tuple(_sb_tile(q_ref[0, h], kn_ref[0, h], vn_ref[0, h], tn_ref[...],
                           jnp.zeros((ts, 1), F32), jnp.zeros((ts, dh), F32), mask)
                  for h in range(nh))
    nblk = past // tkp
    state = past_block(nblk - 1, state)
    for j in reversed(range(nblk - 1)):
        state = lax.cond(_cmin(state) < EXP_UNDERFLOW,
                         functools.partial(past_block, j), lambda st: st, state)
    for h in range(nh):
        o_ref[0, :, h * dh:(h + 1) * dh] = state[h][1]


def _attn_dec(q, kn, vn, kp_t, vp_t, tn, tp, l):
    b, nh, ts, dh = q.shape
    past = kp_t.shape[4]
    new_spec = pl.BlockSpec((1, nh, ts, dh), lambda bi: (bi, 0, 0, 0))
    past_spec = pl.BlockSpec((None, 1, nh, dh, past), lambda bi: (l, bi, 0, 0, 0))
    return pl.pallas_call(
        functools.partial(_attn_dec_kernel, tkp=tp.shape[0]),
        grid=(b,),
        in_specs=[new_spec, new_spec, new_spec, past_spec, past_spec,
                  _const_spec(tn.shape), _const_spec(tp.shape)],
        out_specs=pl.BlockSpec((1, ts, nh * dh), lambda bi: (bi, 0, 0)),
        out_shape=jax.ShapeDtypeStruct((b, ts, nh * dh), F32),
        compiler_params=_params(("parallel",)),
        name="attn_dec",
    )(q, kn, vn, kp_t, vp_t, tn, tp)


def _mix_ffn_kernel(x_ref, att_ref, up_ref, halo_ref, p_ref, gsb_ref, wpool_ref, pscale_ref,
                    wout_ref, gffn_ref, wg_ref, wu_ref, wd_ref, gple_ref, wple_ref, wpg_ref,
                    gfin_ref, o_ref, ext_ref, *, bt, ts, pos_offset, zero_first_halo, final_norm):
    j = pl.program_id(1)
    d = x_ref.shape[-1]
    pw = up_ref.shape[-1]
    gd = pw // len(POOL_WINDOWS)
    pos = pos_offset + j * ts + lax.broadcasted_iota(jnp.int32, (ts, 1), 0)
    mixed = []
    for b in range(bt):
        att = _rms(att_ref[b], gsb_ref[...]).astype(BF16)

        halo = halo_ref[b]
        if zero_first_halo:
            halo = jnp.where(j == 0, 0.0, halo)
        up = up_ref[b]
        ext_ref[0:POOL_HALO, :] = halo
        ext_ref[POOL_HALO:, :] = up
        ys = []
        for g, w in enumerate(POOL_WINDOWS):
            cols = slice(g * gd, (g + 1) * gd)
            win = up[:, cols]
            for sh in range(1, w):
                win = win + ext_ref[POOL_HALO - sh:POOL_HALO - sh + ts, cols]
            cnt = jnp.minimum(w, pos + 1).astype(F32)
            dg = win / cnt - up[:, cols]
            ys.append(jnp.dot(dg.astype(BF16), wpool_ref[g], preferred_element_type=F32))
        pool = _rms(jnp.concatenate(ys, axis=-1), pscale_ref[...]).astype(BF16)

        sbw = att.shape[-1]
        mixed.append(x_ref[b] + jnp.dot(att, wout_ref[:sbw, :], preferred_element_type=F32)
                     + jnp.dot(pool, wout_ref[sbw:, :], preferred_element_type=F32))
    x = mixed[0] if bt == 1 else jnp.concatenate(mixed, axis=0)

    h = _rms(x, gffn_ref[...]).astype(BF16)
    gate = jnp.dot(h, wg_ref[...], preferred_element_type=F32)
    upv = jnp.dot(h, wu_ref[...], preferred_element_type=F32)
    hid = (gate * jax.nn.sigmoid(gate) * upv).astype(BF16)
    x = x + jnp.dot(hid, wd_ref[...], preferred_element_type=F32)
    hp = _rms(x, gple_ref[...]).astype(BF16)
    pg = jax.nn.sigmoid(jnp.dot(hp, wpg_ref[...], preferred_element_type=F32))
    p = p_ref[...].reshape(bt * ts, p_ref.shape[-1]).astype(BF16)
    x = x + jnp.dot(p, wple_ref[...], preferred_element_type=F32) * pg
    if final_norm:
        x = _rms(x, gfin_ref[...])
    o_ref[...] = x.reshape(bt, ts, d)


def _mix_ffn(x, att, up, halo_src, p, gsb, wpool, pscale, wout, gffn, wg, wu, wd, gple, wple, wpg,
             gfin, l, *, bt, ts, pos_offset, zero_first_halo, final_norm):
    b, s, d = x.shape
    sbw = att.shape[-1]
    pw = up.shape[-1]
    pd = p.shape[-1]
    if zero_first_halo:
        halo_spec = pl.BlockSpec((bt, POOL_HALO, pw),
                                 lambda i, j: (i, jnp.maximum(j * (ts // POOL_HALO) - 1, 0), 0))
    else:
        halo_spec = pl.BlockSpec((None, bt, POOL_HALO, pw), lambda i, j: (l, i, 0, 0))
    wspec = functools.partial(_layer_spec, l=l, buffers=1)
    tile = lambda width: pl.BlockSpec((bt, ts, width), lambda i, j: (i, j, 0))
    return pl.pallas_call(
        functools.partial(_mix_ffn_kernel, bt=bt, ts=ts, pos_offset=pos_offset,
                          zero_first_halo=zero_first_halo, final_norm=final_norm),
        grid=(b // bt, s // ts),
        in_specs=[tile(d), tile(sbw), tile(pw), halo_spec,
                  pl.BlockSpec((None, bt, ts, pd), lambda i, j: (l, i, j, 0)),
                  _layer_spec(gsb, l), wspec(wpool), _layer_spec(pscale, l), wspec(wout),
                  _layer_spec(gffn, l), wspec(wg), wspec(wu), wspec(wd),
                  _layer_spec(gple, l), wspec(wple), wspec(wpg), _const_spec(gfin.shape)],
        out_specs=tile(d),
        out_shape=jax.ShapeDtypeStruct((b, s, d), F32),
        scratch_shapes=[pltpu.VMEM((POOL_HALO + ts, pw), F32)],
        compiler_params=_params(("parallel", "arbitrary")),
        name="mix_ffn",
    )(x, att, up, halo_src, p, gsb, wpool, pscale, wout, gffn, wg, wu, wd, gple, wple, wpg, gfin)


def _forward(x_prompt, x_sample, cache_k, cache_v, state_pool, p_prompt, p_sample,
             g_mix, w_in, g_sb_out, w_pool, pool_scale, w_out,
             g_ffn, w_ffn_gate, w_ffn_up, w_ffn_down, g_ple, w_ple, w_ple_gate, g_final,
             *, ts_prompt, tq, hp, nsub, tkp):
    depth = w_in.shape[0]
    bp, sp_len, d = x_prompt.shape
    bs, ss_len, _ = x_sample.shape
    nh, past, dh = cache_k.shape[2:]
    hist = state_pool.shape[2]

    rows = lambda a: a.reshape(depth, 1, -1)
    g_mix, g_sb_out, pool_scale, g_ffn, g_ple = map(rows, (g_mix, g_sb_out, pool_scale, g_ffn, g_ple))
    w_in, w_pool, w_out, wg, wu, wd, wple, wpg = (
        a.astype(BF16) for a in (w_in, w_pool, w_out, w_ffn_gate, w_ffn_up, w_ffn_down,
                                 w_ple, w_ple_gate))
    gfin = g_final.reshape(1, -1)
    t_prompt, t_new, t_past = _strict_lower(tq), _strict_lower(ss_len), _strict_lower(tkp)
    state_pad = jnp.pad(state_pool, ((0, 0), (0, 0), (POOL_HALO - hist, 0), (0, 0)))
    cache_kt, cache_vt = jnp.swapaxes(cache_k, 3, 4), jnp.swapaxes(cache_v, 3, 4)

    xp, xs = x_prompt, x_sample
    kp32 = vp32 = ks32 = vs32 = None
    pool_p, pool_s = [], []
    for l in range(depth):
        last = l == depth - 1
        qp, kp, vp, upp, kp32, vp32 = _in_proj(xp, g_mix, w_in, kp32, vp32, l,
                                               bt=1, ts=ts_prompt, nh=nh, dh=dh)
        attp = _attn(qp, kp, vp, t_prompt, tq=tq, hp=hp, nsub=nsub)
        xp = _mix_ffn(xp, attp, upp, upp, p_prompt, g_sb_out, w_pool, pool_scale, w_out,
                      g_ffn, wg, wu, wd, g_ple, wple, wpg, gfin, l, bt=1, ts=ts_prompt,
                      pos_offset=0, zero_first_halo=True, final_norm=last)
        qs, ks, vs, ups, ks32, vs32 = _in_proj(xs, g_mix, w_in, ks32, vs32, l,
                                               bt=bs, ts=ss_len, nh=nh, dh=dh)
        atts = _attn_dec(qs, ks, vs, cache_kt, cache_vt, t_new, t_past, l)
        xs = _mix_ffn(xs, atts, ups, state_pad, p_sample, g_sb_out, w_pool, pool_scale, w_out,
                      g_ffn, wg, wu, wd, g_ple, wple, wpg, gfin, l, bt=bs, ts=ss_len,
                      pos_offset=past, zero_first_halo=False, final_norm=last)

        pool_p.append(upp[:, -hist:])
        pool_s.append(jnp.concatenate([state_pool[l], ups], axis=1)[:, -hist:])

    return xp, xs, kp32, vp32, jnp.stack(pool_p), ks32, vs32, jnp.stack(pool_s)


def kernel(x_prompt, x_sample, cache_k, cache_v, state_pool, p_prompt, p_sample, g_mix, w_in, g_sb_out, w_pool, pool_scale, w_out, g_ffn, w_ffn_gate, w_ffn_up, w_ffn_down, g_ple, w_ple, w_ple_gate, g_final):
    return _forward(x_prompt, x_sample, cache_k, cache_v, state_pool, p_prompt, p_sample,
                    g_mix, w_in, g_sb_out, w_pool, pool_scale, w_out,
                    g_ffn, w_ffn_gate, w_ffn_up, w_ffn_down, g_ple, w_ple, w_ple_gate, g_final,
                    ts_prompt=512, tq=256, hp=4, nsub=8, tkp=256)
```
